```python
import jax, jax.numpy as jnp
from jax import lax
import numpy as np

D_MODEL = 1024
BATCH = 8
SEQ = 8192
DEPTH = 2

HEAD_DIM = 64
N_Q_HEADS = 8
N_KV_HEADS = 2
GQA_GROUPS = N_Q_HEADS // N_KV_HEADS
WINDOW = 128
BLOCK = 128
ATTN_WIDTH = N_Q_HEADS * HEAD_DIM
KV_WIDTH = N_KV_HEADS * HEAD_DIM
POOL_WINDOWS = (2, 4, 8, 16)
N_POOL_GROUPS = len(POOL_WINDOWS)
POOL_WIDTH = D_MODEL - ATTN_WIDTH
POOL_GROUP = POOL_WIDTH // N_POOL_GROUPS
AB_IN_WIDTH = ATTN_WIDTH + 2 * KV_WIDTH + POOL_WIDTH
AB_OUT_WIDTH = ATTN_WIDTH + POOL_WIDTH
RWKV_HEAD = 64
RWKV_HEADS = D_MODEL // RWKV_HEAD
DECAY_LORA = 64
AAA_LORA = 64
GATE_LORA = 128
N_SHIFT_MIX = 6
D_FF = 2816
N_A_LAYERS = (DEPTH + 1) // 2
N_C_LAYERS = DEPTH // 2
RMS_EPS = 1e-6
GN_EPS = 64e-5

kernel_name = "hybrid_swa_pool_rwkv7_macaron"


def rms_norm(x, gain):
    xf = x.astype(jnp.float32)
    y = xf * lax.rsqrt(jnp.mean(xf * xf, axis=-1, keepdims=True) + RMS_EPS)
    return (y * gain.astype(jnp.float32)).astype(x.dtype)


def swiglu(h, w_gate, w_up, w_down):
    return (jax.nn.silu(h @ w_gate) * (h @ w_up)) @ w_down


def alibi_slopes():
    return 2.0 ** (-8.0 * jnp.arange(1, N_Q_HEADS + 1, dtype=jnp.float32) / N_Q_HEADS)


def sliding_window_attention(q, k, v, sinks):
    b, t = q.shape[:2]
    nb = t // BLOCK
    qb = q.reshape(b, nb, BLOCK, N_KV_HEADS, GQA_GROUPS, HEAD_DIM).astype(jnp.float32)

    def band(z):
        zb = z.reshape(b, nb, BLOCK, N_KV_HEADS, HEAD_DIM)
        prev = jnp.concatenate([jnp.zeros_like(zb[:, :1]), zb[:, :-1]], axis=1)
        return jnp.concatenate([prev, zb], axis=2).astype(jnp.float32)

    kb, vb = band(k), band(v)
    qi = jnp.arange(BLOCK)[:, None]
    kj = jnp.arange(2 * BLOCK)[None, :]
    dist = qi - kj + BLOCK
    key_pos = jnp.arange(nb)[:, None, None] * BLOCK + kj[None] - BLOCK
    valid = (dist >= 0) & (dist < WINDOW) & (key_pos >= 0)
    slopes = alibi_slopes().reshape(N_KV_HEADS, GQA_GROUPS)[None, :, :, None, None, None]
    scores = jnp.einsum('bnqhgd,bnkhd->bhgnqk', qb, kb) * (HEAD_DIM ** -0.5)
    scores = jnp.where(valid, scores - slopes * dist.astype(jnp.float32), -jnp.inf)
    sink = sinks.astype(jnp.float32).reshape(1, N_KV_HEADS, GQA_GROUPS, 1, 1, 1)
    m = jnp.maximum(jnp.max(scores, axis=-1, keepdims=True), sink)
    p = jnp.exp(scores - m)
    denom = jnp.sum(p, axis=-1, keepdims=True) + jnp.exp(sink - m)
    out = jnp.einsum('bhgnqk,bnkhd->bnqhgd', p / denom, vb)
    return out.reshape(b, t, ATTN_WIDTH)


def multiscale_pool(p, pool_w, pool_scale):
    b, t, _ = p.shape
    pf = p.astype(jnp.float32)
    count = jnp.arange(1, t + 1, dtype=jnp.float32)[None, :, None]
    outs = []
    for gi, w in enumerate(POOL_WINDOWS):
        pg = pf[..., gi * POOL_GROUP:(gi + 1) * POOL_GROUP]
        cs = jnp.cumsum(pg, axis=1)
        lag = jnp.concatenate([jnp.zeros_like(cs[:, :w]), cs[:, :-w]], axis=1)
        mean = (cs - lag) / jnp.minimum(count, float(w))
        outs.append(jnp.einsum('btc,cd->btd', mean - pg, pool_w[gi].astype(jnp.float32)))
    return (jnp.concatenate(outs, axis=-1) * pool_scale.astype(jnp.float32)).astype(p.dtype)


def mixer_ab(h, w_in, q_norm, k_norm, sinks, pool_w, pool_scale, w_out):
    b, t, _ = h.shape
    z = h @ w_in
    q = z[..., :ATTN_WIDTH].reshape(b, t, N_Q_HEADS, HEAD_DIM)
    k = z[..., ATTN_WIDTH:ATTN_WIDTH + KV_WIDTH].reshape(b, t, N_KV_HEADS, HEAD_DIM)
    v = z[..., ATTN_WIDTH + KV_WIDTH:ATTN_WIDTH + 2 * KV_WIDTH].reshape(b, t, N_KV_HEADS, HEAD_DIM)
    p = z[..., ATTN_WIDTH + 2 * KV_WIDTH:]
    q = rms_norm(q, q_norm)
    k = rms_norm(k, k_norm)
    o_attn = sliding_window_attention(q, k, v, sinks).astype(h.dtype)
    o_pool = multiscale_pool(p, pool_w, pool_scale)
    return jnp.concatenate([o_attn, o_pool], axis=-1) @ w_out


def wkv7_scan(r, w, k, v, a, bvec):
    bsz, t, nh, n = r.shape

    def step(S, inp):
        r_t, w_t, k_t, v_t, a_t, b_t = inp
        sa = jnp.einsum('bhij,bhj->bhi', S, a_t)
        S = S * w_t[:, :, None, :] + sa[..., None] * b_t[:, :, None, :] + v_t[..., None] * k_t[:, :, None, :]
        return S, jnp.einsum('bhij,bhj->bhi', S, r_t)

    S0 = jnp.zeros((bsz, nh, n, n), jnp.float32)
    xs = tuple(jnp.swapaxes(z, 0, 1) for z in (r, w, k, v, a, bvec))
    _, y = lax.scan(step, S0, xs)
    return jnp.swapaxes(y, 0, 1)


def rwkv7_time_mix(h, mu, w_r, w_k, w_v, w0, w1, w2, a0, a1, a2, g1, g2,
                   k_k, k_a, r_k, lnx_w, lnx_b, w_o):
    b, t, d = h.shape
    xx = jnp.concatenate([jnp.zeros_like(h[:, :1]), h[:, :-1]], axis=1) - h
    xr, xw, xk, xv, xa, xg = (h + xx * mu[i] for i in range(N_SHIFT_MIX))
    r = xr @ w_r
    k = xk @ w_k
    v = xv @ w_v
    w = -jax.nn.softplus(-(w0 + jnp.tanh(xw @ w1) @ w2)) - 0.5
    a = jax.nn.sigmoid(a0 + (xa @ a1) @ a2)
    g = jax.nn.sigmoid(xg @ g1) @ g2

    def heads(z):
        return z.reshape(b, t, RWKV_HEADS, RWKV_HEAD).astype(jnp.float32)

    kk = heads(k * k_k)
    kk = kk * lax.rsqrt(jnp.maximum(jnp.sum(kk * kk, axis=-1, keepdims=True), 1e-24))
    k = k * (1 + (a - 1) * k_a)
    r_h, k_h, v_h, a_h = heads(r), heads(k), heads(v), heads(a)
    decay = jnp.exp(-jnp.exp(heads(w)))
    y = wkv7_scan(r_h, decay, k_h, v_h, -kk, kk * a_h)
    mean = jnp.mean(y, axis=-1, keepdims=True)
    var = jnp.mean(jnp.square(y - mean), axis=-1, keepdims=True)
    y = ((y - mean) * lax.rsqrt(var + GN_EPS)).reshape(b, t, d)
    y = y * lnx_w.astype(jnp.float32) + lnx_b.astype(jnp.float32)
    bonus = jnp.sum(r_h * k_h * r_k.astype(jnp.float32), axis=-1, keepdims=True) * v_h
    y = y + bonus.reshape(b, t, d)
    return (y * g.astype(jnp.float32)).astype(h.dtype) @ w_o


def setup_inputs(seed: int = 0) -> dict:
    key = jax.random.key(seed)
    ks = iter(jax.random.split(key, 40))
    f32 = jnp.float32

    def nrm(shape, scale):
        return jax.random.normal(next(ks), shape, f32) * scale

    def gain(shape):
        return 1.0 + nrm(shape, 0.02)

    def unif(shape, lo, hi):
        return jax.random.uniform(next(ks), shape, f32, lo, hi)

    D, F, NA, NC = D_MODEL, D_FF, N_A_LAYERS, N_C_LAYERS
    return {
        "x": nrm((BATCH, SEQ, D), 1.0),
        "ffn_norm": gain((DEPTH, 2, D)),
        "ffn_w_gate": nrm((DEPTH, 2, D, F), D ** -0.5),
        "ffn_w_up": nrm((DEPTH, 2, D, F), D ** -0.5),
        "ffn_w_down": nrm((DEPTH, 2, F, D), F ** -0.5),
        "ab_norm": gain((NA, D)),
        "ab_w_in": nrm((NA, D, AB_IN_WIDTH), D ** -0.5),
        "q_norm": gain((NA, HEAD_DIM)),
        "k_norm": gain((NA, HEAD_DIM)),
        "attn_sinks": nrm((NA, N_Q_HEADS), 1.0),
        "pool_w": nrm((NA, N_POOL_GROUPS, POOL_GROUP, POOL_GROUP), POOL_GROUP ** -0.5),
        "pool_scale": 0.5 + nrm((NA, POOL_WIDTH), 0.05),
        "ab_w_out": nrm((NA, AB_OUT_WIDTH, D), AB_OUT_WIDTH ** -0.5),
        "c_norm": gain((NC, D)),
        "c_mu": unif((NC, N_SHIFT_MIX, D), 0.0, 1.0),
        "c_w_r": nrm((NC, D, D), D ** -0.5),
        "c_w_k": nrm((NC, D, D), D ** -0.5),
        "c_w_v": nrm((NC, D, D), D ** -0.5),
        "c_w0": unif((NC, D), -5.0, 1.0),
        "c_w1": nrm((NC, D, DECAY_LORA), D ** -0.5),
        "c_w2": nrm((NC, DECAY_LORA, D), 0.1 * DECAY_LORA ** -0.5),
        "c_a0": nrm((NC, D), 0.1),
        "c_a1": nrm((NC, D, AAA_LORA), D ** -0.5),
        "c_a2": nrm((NC, AAA_LORA, D), 0.1 * AAA_LORA ** -0.5),
        "c_g1": nrm((NC, D, GATE_LORA), D ** -0.5),
        "c_g2": nrm((NC, GATE_LORA, D), GATE_LORA ** -0.5),
        "c_k_k": 0.85 + nrm((NC, D), 0.02),
        "c_k_a": 1.0 + nrm((NC, D), 0.02),
        "c_r_k": nrm((NC, RWKV_HEADS, RWKV_HEAD), 0.1),
        "c_lnx_w": gain((NC, D)),
        "c_lnx_b": nrm((NC, D), 0.02),
        "c_w_o": nrm((NC, D, D), D ** -0.5),
    }


def reference(x, ffn_norm, ffn_w_gate, ffn_w_up, ffn_w_down,
              ab_norm, ab_w_in, q_norm, k_norm, attn_sinks, pool_w, pool_scale, ab_w_out,
              c_norm, c_mu, c_w_r, c_w_k, c_w_v, c_w0, c_w1, c_w2, c_a0, c_a1, c_a2,
              c_g1, c_g2, c_k_k, c_k_a, c_r_k, c_lnx_w, c_lnx_b, c_w_o):
    for layer in range(DEPTH):
        x = x + 0.5 * swiglu(rms_norm(x, ffn_norm[layer, 0]), ffn_w_gate[layer, 0],
                             ffn_w_up[layer, 0], ffn_w_down[layer, 0])
        j = layer // 2
        if layer % 2 == 0:
            x = x + mixer_ab(rms_norm(x, ab_norm[j]), ab_w_in[j], q_norm[j], k_norm[j],
                             attn_sinks[j], pool_w[j], pool_scale[j], ab_w_out[j])
        else:
            x = x + rwkv7_time_mix(rms_norm(x, c_norm[j]), c_mu[j], c_w_r[j], c_w_k[j], c_w_v[j],
                                   c_w0[j], c_w1[j], c_w2[j], c_a0[j], c_a1[j], c_a2[j],
                                   c_g1[j], c_g2[j], c_k_k[j], c_k_a[j], c_r_k[j],
                                   c_lnx_w[j], c_lnx_b[j], c_w_o[j])
        x = x + 0.5 * swiglu(rms_norm(x, ffn_norm[layer, 1]), ffn_w_gate[layer, 1],
                             ffn_w_up[layer, 1], ffn_w_down[layer, 1])
    return x
```

```python
import functools

import jax
import jax.numpy as jnp
from jax import lax
from jax.experimental import pallas as pl
from jax.experimental.pallas import tpu as pltpu

F32 = jnp.float32
BF16 = jnp.bfloat16

D_MODEL = 1024
D_FF = 2816
HEAD_DIM = 64
N_Q_HEADS = 8
N_KV_HEADS = 2
GQA_GROUPS = N_Q_HEADS // N_KV_HEADS
WINDOW = 128
ATTN_WIDTH = N_Q_HEADS * HEAD_DIM
KV_WIDTH = N_KV_HEADS * HEAD_DIM
POOL_WINDOWS = (2, 4, 8, 16)
POOL_GROUP = 128
POOL_WIDTH = len(POOL_WINDOWS) * POOL_GROUP
AB_IN_WIDTH = ATTN_WIDTH + 2 * KV_WIDTH + POOL_WIDTH
POOL_HALO = 16
RWKV_HEAD = 64
RMS_EPS = 1e-6
GN_EPS = 64e-5

MXU_TILE = 256
HEADS_PER_QUAD = MXU_TILE // RWKV_HEAD
N_QUADS = D_MODEL // MXU_TILE
WKV_CHUNK = 64
VMEM_LIMIT_BYTES = 56 * 1024 * 1024

FFN_TILE = 512
FFN_F_CHUNK = 256
AB_TILE = 512
RWKV_TILE = 256


def _dot(a, b):
    return jnp.dot(a, b, preferred_element_type=F32)


def _dot_nt(a, b):
    return lax.dot_general(a, b, (((1,), (1,)), ((), ())), preferred_element_type=F32)


def _dot_tn(a, b):
    return lax.dot_general(a, b, (((0,), (0,)), ((), ())), preferred_element_type=F32)


def _rms_norm(x, gain):
    return x * lax.rsqrt(jnp.mean(x * x, axis=-1, keepdims=True) + RMS_EPS) * gain


def _head_sum_matrix(width, scale):
    r = lax.broadcasted_iota(jnp.int32, (width, width), 0) >> 6
    c = lax.broadcasted_iota(jnp.int32, (width, width), 1) >> 6
    return jnp.where(r == c, scale, 0.0).astype(BF16)


def _head_sums(z, hs):
    parts = [_dot(z[:, i:i + MXU_TILE].astype(BF16), hs) for i in range(0, z.shape[1], MXU_TILE)]
    return parts[0] if len(parts) == 1 else jnp.concatenate(parts, axis=1)


def _ffn_body(x_ref, gain_ref, wg_ref, wu_ref, wd_ref, o_ref):
    x = x_ref[...]
    h = _rms_norm(x, gain_ref[...]).astype(BF16)
    acc = jnp.zeros(x.shape, F32)
    for c in range(0, D_FF, FFN_F_CHUNK):
        gate = _dot(h, wg_ref[:, c:c + FFN_F_CHUNK])
        up = _dot(h, wu_ref[:, c:c + FFN_F_CHUNK])
        act = (jax.nn.silu(gate) * up).astype(BF16)
        acc = acc + _dot(act, wd_ref[c:c + FFN_F_CHUNK, :])
    o_ref[...] = x + 0.5 * acc


def _ffn_call(x2, gain, wg, wu, wd, layer, half):
    n = x2.shape[0]
    tile = min(FFN_TILE, n)
    w_idx = lambda i: (layer, half, 0, 0)
    return pl.pallas_call(
        _ffn_body,
        grid=(n // tile,),
        in_specs=[
            pl.BlockSpec((tile, D_MODEL), lambda i: (i, 0)),
            pl.BlockSpec((None, None, 1, D_MODEL), w_idx),
            pl.BlockSpec((None, None, D_MODEL, D_FF), w_idx),
            pl.BlockSpec((None, None, D_MODEL, D_FF), w_idx),
            pl.BlockSpec((None, None, D_FF, D_MODEL), w_idx),
        ],
        out_specs=pl.BlockSpec((tile, D_MODEL), lambda i: (i, 0)),
        out_shape=jax.ShapeDtypeStruct(x2.shape, F32),
        compiler_params=pltpu.CompilerParams(
            dimension_semantics=("parallel",), vmem_limit_bytes=VMEM_LIMIT_BYTES),
        name=f"ffn_l{layer}_h{half}",
    )(x2, gain, wg, wu, wd)


def _attention_block(q_blk, k_cat, v_cat, sinks_ref, first_key):
    qi = lax.broadcasted_iota(jnp.int32, (WINDOW, 2 * WINDOW), 0)
    kj = lax.broadcasted_iota(jnp.int32, (WINDOW, 2 * WINDOW), 1)
    dist = qi - kj + WINDOW
    valid = (dist >= 0) & (dist < WINDOW) & (kj >= first_key)
    dist_f = dist.astype(F32)
    outs = []
    for g in range(N_KV_HEADS):
        k_g = k_cat[:, g * HEAD_DIM:(g + 1) * HEAD_DIM]
        v_g = v_cat[:, g * HEAD_DIM:(g + 1) * HEAD_DIM]
        heads = range(g * GQA_GROUPS, (g + 1) * GQA_GROUPS)
        q4 = jnp.concatenate([q_blk[:, h * HEAD_DIM:(h + 1) * HEAD_DIM] for h in heads], axis=0).astype(BF16)
        s = _dot_nt(q4, k_g) * (HEAD_DIM ** -0.5)
        bias = jnp.concatenate(
            [jnp.where(valid, -(2.0 ** -(h + 1)) * dist_f, -jnp.inf) for h in heads], axis=0)
        s = s + bias
        sink = jnp.concatenate([jnp.full((WINDOW, 1), sinks_ref[h], F32) for h in heads], axis=0)
        m = jnp.maximum(jnp.max(s, axis=-1, keepdims=True), sink)
        p = jnp.exp(s - m)
        denom = jnp.sum(p, axis=-1, keepdims=True) + jnp.exp(sink - m)
        o = _dot(p.astype(BF16), v_g) / denom
        outs += [o[i * WINDOW:(i + 1) * WINDOW] for i in range(GQA_GROUPS)]
    return jnp.concatenate(outs, axis=1)


def _pool_block(p_ext, pos0, pool_w_ref, pool_scale):
    tile = p_ext.shape[0] - POOL_HALO
    count = (lax.broadcasted_iota(jnp.int32, (tile, 1), 0) + pos0 + 1).astype(F32)
    outs = []
    for gi, w in enumerate(POOL_WINDOWS):
        pe = p_ext[:, gi * POOL_GROUP:(gi + 1) * POOL_GROUP]
        acc, span = pe, 1
        while span < w:
            acc = acc[span:] + acc[:-span]
            span *= 2
        win = acc[acc.shape[0] - tile:]
        pg = pe[POOL_HALO:]
        mean = win / jnp.minimum(count, float(w))
        outs.append(_dot((mean - pg).astype(BF16), pool_w_ref[gi]))
    return jnp.concatenate(outs, axis=1) * pool_scale


def _mixer_ab_body(x_ref, gain_ref, w_in_ref, qn_ref, kn_ref, sinks_ref, pool_w_ref, pool_scale_ref, w_out_ref,
                   o_ref, k_ext, v_ext, p_ext):
    t = pl.program_id(1)
    tile = x_ref.shape[0]

    @pl.when(t == 0)
    def _():
        k_ext[0:WINDOW, :] = jnp.zeros((WINDOW, KV_WIDTH), BF16)
        v_ext[0:WINDOW, :] = jnp.zeros((WINDOW, KV_WIDTH), BF16)
        p_ext[0:POOL_HALO, :] = jnp.zeros((POOL_HALO, POOL_WIDTH), F32)

    x = x_ref[...]
    h = _rms_norm(x, gain_ref[...]).astype(BF16)
    z = _dot(h, w_in_ref[...])
    q = z[:, :ATTN_WIDTH]
    k = z[:, ATTN_WIDTH:ATTN_WIDTH + KV_WIDTH]
    v = z[:, ATTN_WIDTH + KV_WIDTH:ATTN_WIDTH + 2 * KV_WIDTH]
    p = z[:, ATTN_WIDTH + 2 * KV_WIDTH:]

    hs = _head_sum_matrix(MXU_TILE, 1.0 / HEAD_DIM)
    q = q * lax.rsqrt(_head_sums(q * q, hs) + RMS_EPS) * qn_ref[...]
    k = k * lax.rsqrt(_dot((k * k).astype(BF16), hs[:KV_WIDTH, :KV_WIDTH]) + RMS_EPS) * kn_ref[...]

    k_ext[WINDOW:, :] = k.astype(BF16)
    v_ext[WINDOW:, :] = v.astype(BF16)
    p_ext[POOL_HALO:, :] = p

    attn = []
    for j in range(tile // WINDOW):
        first_key = jnp.where(t == 0, WINDOW, 0) if j == 0 else 0
        attn.append(_attention_block(
            q[j * WINDOW:(j + 1) * WINDOW], k_ext[j * WINDOW:(j + 2) * WINDOW, :],
            v_ext[j * WINDOW:(j + 2) * WINDOW, :], sinks_ref, first_key))
    o_attn = jnp.concatenate(attn, axis=0)
    o_pool = _pool_block(p_ext[...], t * tile, pool_w_ref, pool_scale_ref[...])

    mixed = jnp.concatenate([o_attn, o_pool], axis=1).astype(BF16)
    o_ref[...] = x + _dot(mixed, w_out_ref[...])

    k_ext[0:WINDOW, :] = k_ext[tile:tile + WINDOW, :]
    v_ext[0:WINDOW, :] = v_ext[tile:tile + WINDOW, :]
    p_ext[0:POOL_HALO, :] = p_ext[tile:tile + POOL_HALO, :]


def _mixer_ab_call(x, gain, w_in, q_gain, k_gain, sinks, pool_w, pool_scale, w_out):
    b, t, _ = x.shape
    tile = min(AB_TILE, t)
    full = lambda *shape: pl.BlockSpec(shape, lambda i, j: (0,) * len(shape))
    return pl.pallas_call(
        _mixer_ab_body,
        grid=(b, t // tile),
        in_specs=[
            pl.BlockSpec((None, tile, D_MODEL), lambda i, j: (i, j, 0)),
            full(1, D_MODEL),
            full(D_MODEL, AB_IN_WIDTH),
            full(1, ATTN_WIDTH),
            full(1, KV_WIDTH),
            pl.BlockSpec(memory_space=pltpu.SMEM),
            full(len(POOL_WINDOWS), POOL_GROUP, POOL_GROUP),
            full(1, POOL_WIDTH),
            full(ATTN_WIDTH + POOL_WIDTH, D_MODEL),
        ],
        out_specs=pl.BlockSpec((None, tile, D_MODEL), lambda i, j: (i, j, 0)),
        out_shape=jax.ShapeDtypeStruct(x.shape, F32),
        scratch_shapes=[
            pltpu.VMEM((WINDOW + tile, KV_WIDTH), BF16),
            pltpu.VMEM((WINDOW + tile, KV_WIDTH), BF16),
            pltpu.VMEM((POOL_HALO + tile, POOL_WIDTH), F32),
        ],
        compiler_params=pltpu.CompilerParams(
            dimension_semantics=("arbitrary", "arbitrary"), vmem_limit_bytes=VMEM_LIMIT_BYTES),
        name="mixer_ab",
    )(x, gain, w_in, q_gain, k_gain, sinks, pool_w, pool_scale, w_out)


def _stack_heads(x, block_diag):
    xb = x.astype(BF16)
    return jnp.where(block_diag, jnp.concatenate([xb] * HEADS_PER_QUAD, axis=0), jnp.zeros((), BF16))


INV_BASE = 8


def _inverse_masks():
    row = lax.broadcasted_iota(jnp.int32, (WKV_CHUNK, MXU_TILE), 0)
    col = lax.broadcasted_iota(jnp.int32, (WKV_CHUNK, MXU_TILE), 1) & (WKV_CHUNK - 1)
    same_block = lambda size: (row // size) == (col // size)
    levels, size = [], INV_BASE
    while size < WKV_CHUNK:
        levels.append(same_block(2 * size) & jnp.logical_not(same_block(size)))
        size *= 2
    return row == col, same_block(INV_BASE), tuple(levels)


def _unit_lower_inverse(n, stack, inv_masks):
    eye, base_blocks, levels = inv_masks
    mm = lambda a, b: _dot(a.astype(BF16), stack(b))
    nd = jnp.where(base_blocks, n, 0.0)
    inv = jnp.where(eye, 1.0, 0.0) + nd
    power, span = nd, 1
    while 2 * span < INV_BASE:
        power = mm(power, power)
        inv = inv + mm(inv, power)
        span *= 2
    for off_blocks in levels:
        n_off = jnp.where(off_blocks, n, 0.0)
        inv = inv + mm(inv, mm(n_off, inv))
    return inv


def _wkv_quad_chunk(r_t, a_t, k_t, b_t, v, k_end, b_end, decay_end, state, masks):
    block_diag, strict, incl, inv_masks = masks
    stack = functools.partial(_stack_heads, block_diag=block_diag)
    lhs = jnp.concatenate([a_t, r_t], axis=0).astype(BF16)
    with_k = _dot_nt(lhs, stack(k_t))
    with_b = _dot_nt(lhs, stack(b_t))
    a_ak = jnp.where(strict, with_k[:WKV_CHUNK], 0.0)
    a_rk = jnp.where(incl, with_k[WKV_CHUNK:], 0.0)
    a_ab = jnp.where(strict, with_b[:WKV_CHUNK], 0.0)
    a_rb = jnp.where(incl, with_b[WKV_CHUNK:], 0.0)
    inv = _unit_lower_inverse(a_ab, stack, inv_masks)
    times_v = _dot(jnp.concatenate([a_ak, a_rk], axis=0).astype(BF16), stack(v))
    inv_b = inv.astype(BF16)
    u_local = _dot(inv_b, stack(times_v[:WKV_CHUNK]))
    w = _dot(inv_b, stack(a_t))
    from_state = _dot_nt(jnp.concatenate([w, r_t], axis=0).astype(BF16), state.astype(BF16))
    u = from_state[:WKV_CHUNK] + u_local
    y = from_state[WKV_CHUNK:] + times_v[WKV_CHUNK:] + _dot(a_rb.astype(BF16), stack(u))
    outer = _dot_tn(jnp.concatenate([v, u], axis=0).astype(BF16),
                    jnp.concatenate([k_end, b_end], axis=0).astype(BF16))
    new_state = state * decay_end + jnp.where(block_diag, outer, 0.0)
    return y, new_state


def _rwkv_body(x_ref, gain_ref, mu_ref, wr_ref, wk_ref, wv_ref, w0_ref, w1_ref, w2_ref, a0_ref, a1_ref, a2_ref,
               g1_ref, g2_ref, kk_ref, ka_ref, rk_ref, lnw_ref, lnb_ref, wo_ref, o_ref,
               hprev, state, r_s, k_s, v_s, lw_s, kk_s, a_s, y_s):
    t = pl.program_id(1)
    tile = x_ref.shape[0]

    @pl.when(t == 0)
    def _():
        hprev[...] = jnp.zeros(hprev.shape, F32)
        state[...] = jnp.zeros(state.shape, F32)

    x = x_ref[...]
    h = _rms_norm(x, gain_ref[...])
    row = lax.broadcasted_iota(jnp.int32, (tile, 1), 0)
    shifted = jnp.where(row == 0, hprev[0:1, :], pltpu.roll(h, 1, axis=0))
    hprev[0:1, :] = h[tile - 1:tile, :]
    xx = shifted - h
    mix = lambda i: (h + xx * mu_ref[i:i + 1, :]).astype(BF16)
    r = _dot(mix(0), wr_ref[...])
    k = _dot(mix(2), wk_ref[...])
    v = _dot(mix(3), wv_ref[...])
    w_log = -jax.nn.softplus(-(w0_ref[...] + _dot(jnp.tanh(_dot(mix(1), w1_ref[...])).astype(BF16), w2_ref[...]))) - 0.5
    a = jax.nn.sigmoid(a0_ref[...] + _dot(_dot(mix(4), a1_ref[...]).astype(BF16), a2_ref[...]))
    g = _dot(jax.nn.sigmoid(_dot(mix(5), g1_ref[...])).astype(BF16), g2_ref[...])

    hs = _head_sum_matrix(MXU_TILE, 1.0)
    kk = k * kk_ref[...]
    kk = kk * lax.rsqrt(jnp.maximum(_head_sums(kk * kk, hs), 1e-24))
    k = k * (1.0 + (a - 1.0) * ka_ref[...])
    r_s[...] = r
    k_s[...] = k
    v_s[...] = v
    lw_s[...] = -jnp.exp(w_log)
    kk_s[...] = kk
    a_s[...] = a

    rr = lax.broadcasted_iota(jnp.int32, (MXU_TILE, MXU_TILE), 0) >> 6
    cc = lax.broadcasted_iota(jnp.int32, (MXU_TILE, MXU_TILE), 1) >> 6
    block_diag = rr == cc
    ti = lax.broadcasted_iota(jnp.int32, (WKV_CHUNK, MXU_TILE), 0)
    tj = lax.broadcasted_iota(jnp.int32, (WKV_CHUNK, MXU_TILE), 1) & (WKV_CHUNK - 1)
    masks = (block_diag, tj < ti, tj <= ti, _inverse_masks())
    ci = lax.broadcasted_iota(jnp.int32, (WKV_CHUNK, WKV_CHUNK), 0)
    cj = lax.broadcasted_iota(jnp.int32, (WKV_CHUNK, WKV_CHUNK), 1)
    prefix = jnp.where(cj <= ci, 1.0, 0.0).astype(BF16)

    def chunk(c, carry):
        rows = pl.ds(pl.multiple_of(c * WKV_CHUNK, WKV_CHUNK), WKV_CHUNK)
        lw = lw_s[rows, :]
        lw_hi = lw.astype(BF16)
        lw_mid = (lw - lw_hi.astype(F32)).astype(BF16)
        lw_lo = (lw - lw_hi.astype(F32) - lw_mid.astype(F32)).astype(BF16)
        cum = _dot(prefix, lw_hi) + _dot(prefix, lw_mid) + _dot(prefix, lw_lo)
        total = cum[WKV_CHUNK - 1:WKV_CHUNK, :]
        d_in = jnp.exp(cum)
        d_ex = jnp.exp(cum - lw)
        d_inv = jnp.exp(-cum)
        d_end = jnp.exp(total - cum)
        kk_c = kk_s[rows, :]
        k_c = k_s[rows, :]
        b_c = kk_c * a_s[rows, :]
        r_t = r_s[rows, :] * d_in
        a_t = -kk_c * d_ex
        k_t = k_c * d_inv
        b_t = b_c * d_inv
        k_end = k_c * d_end
        b_end = b_c * d_end
        v_c = v_s[rows, :]
        decay_end = jnp.exp(total)
        for q in range(N_QUADS):
            lanes = slice(q * MXU_TILE, (q + 1) * MXU_TILE)
            y, new_state = _wkv_quad_chunk(
                r_t[:, lanes], a_t[:, lanes], k_t[:, lanes], b_t[:, lanes], v_c[:, lanes],
                k_end[:, lanes], b_end[:, lanes], decay_end[:, lanes], state[q], masks)
            state[q] = new_state
            y_s[rows, lanes] = y
        return carry

    lax.fori_loop(0, tile // WKV_CHUNK, chunk, 0)

    y = y_s[...]
    r = r_s[...]
    k = k_s[...]
    v = v_s[...]
    mean = _head_sums(y, hs) * (1.0 / RWKV_HEAD)
    cen = y - mean
    var = _head_sums(cen * cen, hs) * (1.0 / RWKV_HEAD)
    yn = cen * lax.rsqrt(var + GN_EPS) * lnw_ref[...] + lnb_ref[...]
    bonus = _head_sums(r * k * rk_ref[...], hs) * v
    out = ((yn + bonus) * g).astype(BF16)
    o_ref[...] = x + _dot(out, wo_ref[...])


def _rwkv_call(x, gain, mu, wr, wk, wv, w0, w1, w2, a0, a1, a2, g1, g2, k_k, k_a, r_k, lnw, lnb, wo):
    b, t, _ = x.shape
    tile = min(RWKV_TILE, t)
    full = lambda *shape: pl.BlockSpec(shape, lambda i, j: (0,) * len(shape))
    vec = full(1, D_MODEL)
    sq = full(D_MODEL, D_MODEL)
    act = lambda: pltpu.VMEM((tile, D_MODEL), F32)
    return pl.pallas_call(
        _rwkv_body,
        grid=(b, t // tile),
        in_specs=[
            pl.BlockSpec((None, tile, D_MODEL), lambda i, j: (i, j, 0)),
            vec, full(mu.shape[0], D_MODEL), sq, sq, sq,
            vec, full(*w1.shape), full(*w2.shape),
            vec, full(*a1.shape), full(*a2.shape),
            full(*g1.shape), full(*g2.shape),
            vec, vec, vec, vec, vec, sq,
        ],
        out_specs=pl.BlockSpec((None, tile, D_MODEL), lambda i, j: (i, j, 0)),
        out_shape=jax.ShapeDtypeStruct(x.shape, F32),
        scratch_shapes=[
            pltpu.VMEM((8, D_MODEL), F32),
            pltpu.VMEM((N_QUADS, MXU_TILE, MXU_TILE), F32),
            act(), act(), act(), act(), act(), act(), act(),
        ],
        compiler_params=pltpu.CompilerParams(
            dimension_semantics=("arbitrary", "arbitrary"), vmem_limit_bytes=VMEM_LIMIT_BYTES),
        name="rwkv7",
    )(x, gain, mu, wr, wk, wv, w0, w1, w2, a0, a1, a2, g1, g2, k_k, k_a, r_k, lnw, lnb, wo)


def kernel(x, ffn_norm, ffn_w_gate, ffn_w_up, ffn_w_down, ab_norm, ab_w_in, q_norm, k_norm, attn_sinks, pool_w, pool_scale, ab_w_out, c_norm, c_mu, c_w_r, c_w_k, c_w_v, c_w0, c_w1, c_w2, c_a0, c_a1, c_a2, c_g1, c_g2, c_k_k, c_k_a, c_r_k, c_lnx_w, c_lnx_b, c_w_o):
    b, t, d = x.shape
    depth = ffn_norm.shape[0]
    bf = lambda w: w.astype(BF16)
    row = lambda p: p.reshape(1, -1).astype(F32)
    ffn_gain = ffn_norm.reshape(depth, 2, 1, d)
    wg, wu, wd = bf(ffn_w_gate), bf(ffn_w_up), bf(ffn_w_down)

    def ffn(x, layer, half):
        return _ffn_call(x.reshape(b * t, d), ffn_gain, wg, wu, wd, layer, half).reshape(b, t, d)

    for layer in range(depth):
        x = ffn(x, layer, 0)
        j = layer // 2
        if layer % 2 == 0:
            x = _mixer_ab_call(
                x, row(ab_norm[j]), bf(ab_w_in[j]), row(jnp.tile(q_norm[j], N_Q_HEADS)),
                row(jnp.tile(k_norm[j], N_KV_HEADS)), attn_sinks[j].astype(F32), bf(pool_w[j]),
                row(pool_scale[j]), bf(ab_w_out[j]))
        else:
            x = _rwkv_call(
                x, row(c_norm[j]), c_mu[j].astype(F32), bf(c_w_r[j]), bf(c_w_k[j]), bf(c_w_v[j]),
                row(c_w0[j]), bf(c_w1[j]), bf(c_w2[j]), row(c_a0[j]), bf(c_a1[j]), bf(c_a2[j]),
                bf(c_g1[j]), bf(c_g2[j]), row(c_k_k[j]), row(c_k_a[j]), row(c_r_k[j]),
                row(c_lnx_w[j]), row(c_lnx_b[j]), bf(c_w_o[j]))
        x = ffn(x, layer, 1)
    return x
```

```python
import jax
import jax.numpy as jnp
from jax import lax
from jax.experimental import pallas as pl
from jax.experimental.pallas import tpu as pltpu

F32 = jnp.float32
BF16 = jnp.bfloat16

D_MODEL = 1024
D_FF = 2816
HEAD_DIM = 64
N_Q_HEADS = 8
N_KV_HEADS = 2
GQA_GROUPS = N_Q_HEADS // N_KV_HEADS
WINDOW = 128
ATTN_WIDTH = N_Q_HEADS * HEAD_DIM
KV_WIDTH = N_KV_HEADS * HEAD_DIM
POOL_WINDOWS = (2, 4, 8, 16)
POOL_GROUP = 128
POOL_WIDTH = len(POOL_WINDOWS) * POOL_GROUP
AB_IN_WIDTH = ATTN_WIDTH + 2 * KV_WIDTH + POOL_WIDTH
POOL_HALO = 16
RWKV_HEAD = 64
RMS_EPS = 1e-6
GN_EPS = 64e-5

MXU_TILE = 256
HEADS_PER_QUAD = MXU_TILE // RWKV_HEAD
N_QUADS = D_MODEL // MXU_TILE
WKV_CHUNK = 64
VMEM_LIMIT_BYTES = 56 * 1024 * 1024

FFN_TILE = 512
FFN_F_CHUNK = 256
AB_TILE = 512
RWKV_TILE = 256


def _dot(a, b):
    return jnp.dot(a, b, preferred_element_type=F32)


def _dot_nt(a, b):
    return lax.dot_general(a, b, (((1,), (1,)), ((), ())), preferred_element_type=F32)


def _dot_tn(a, b):
    return lax.dot_general(a, b, (((0,), (0,)), ((), ())), preferred_element_type=F32)


def _rms_norm(x, gain):
    return x * lax.rsqrt(jnp.mean(x * x, axis=-1, keepdims=True) + RMS_EPS) * gain


def _head_sum_matrix(width, scale):
    r = lax.broadcasted_iota(jnp.int32, (width, width), 0) >> 6
    c = lax.broadcasted_iota(jnp.int32, (width, width), 1) >> 6
    return jnp.where(r == c, scale, 0.0).astype(BF16)


def _head_sums(z, hs):
    parts = [_dot(z[:, i:i + MXU_TILE].astype(BF16), hs) for i in range(0, z.shape[1], MXU_TILE)]
    return parts[0] if len(parts) == 1 else jnp.concatenate(parts, axis=1)


def _ffn_body(x_ref, gain_ref, wg_ref, wu_ref, wd_ref, o_ref):
    x = x_ref[...]
    h = _rms_norm(x, gain_ref[...]).astype(BF16)
    acc = jnp.zeros(x.shape, F32)
    for c in range(0, D_FF, FFN_F_CHUNK):
        gate = _dot(h, wg_ref[:, c:c + FFN_F_CHUNK])
        up = _dot(h, wu_ref[:, c:c + FFN_F_CHUNK])
        act = (jax.nn.silu(gate) * up).astype(BF16)
        acc = acc + _dot(act, wd_ref[c:c + FFN_F_CHUNK, :])
    o_ref[...] = x + 0.5 * acc


def _ffn_call(x2, gain, wg, wu, wd, layer, half):
    n = x2.shape[0]
    tile = min(FFN_TILE, n)
    w_idx = lambda i: (layer, half, 0, 0)
    return pl.pallas_call(
        _ffn_body,
        grid=(n // tile,),
        in_specs=[
            pl.BlockSpec((tile, D_MODEL), lambda i: (i, 0)),
            pl.BlockSpec((None, None, 1, D_MODEL), w_idx),
            pl.BlockSpec((None, None, D_MODEL, D_FF), w_idx),
            pl.BlockSpec((None, None, D_MODEL, D_FF), w_idx),
            pl.BlockSpec((None, None, D_FF, D_MODEL), w_idx),
        ],
        out_specs=pl.BlockSpec((tile, D_MODEL), lambda i: (i, 0)),
        out_shape=jax.ShapeDtypeStruct(x2.shape, F32),
        compiler_params=pltpu.CompilerParams(
            dimension_semantics=("parallel",), vmem_limit_bytes=VMEM_LIMIT_BYTES),
        name=f"ffn_l{layer}_h{half}",
    )(x2, gain, wg, wu, wd)


def _alibi_bias(g):
    shape = (2 * 2 * WINDOW, 2 * WINDOW)
    row = lax.broadcasted_iota(jnp.int32, shape, 0)
    lane = lax.broadcasted_iota(jnp.int32, shape, 1)
    dist = (lane & (WINDOW - 1)) - (row & (2 * WINDOW - 1)) + WINDOW
    slope = (2.0 ** -(GQA_GROUPS * g + 1)) * jnp.where(lane >= WINDOW, 0.25, 1.0) * jnp.where(
        row >= 2 * WINDOW, 0.5, 1.0)
    return jnp.where((dist >= 0) & (dist < WINDOW), -slope * dist.astype(F32), -jnp.inf)


def _attention(q, kz, vt_ext, bias_ref, sinks_ref, first_key):
    n_blocks = q.shape[0] // WINDOW
    probs = [(j, g) for j in range(n_blocks) for g in range(N_KV_HEADS)]
    halves = [slice(0, 2 * WINDOW), slice(2 * WINDOW, 4 * WINDOW)]
    lane = lax.broadcasted_iota(jnp.int32, (1, 2 * WINDOW), 1)
    sinks = {(g, par): jnp.where(lane < WINDOW, sinks_ref[GQA_GROUPS * g + par], sinks_ref[GQA_GROUPS * g + 2 + par])
             for g in range(N_KV_HEADS) for par in range(2)}
    key_row = lax.broadcasted_iota(jnp.int32, (2 * 2 * WINDOW, 2 * WINDOW), 0) & (2 * WINDOW - 1)
    before_start = jnp.where(key_row >= first_key, 0.0, -jnp.inf)

    def key_window(ref, j):
        return ref[j * WINDOW:(j + 2) * WINDOW, :]

    keys = [jnp.concatenate([key_window(kz[2 * g], j), key_window(kz[2 * g + 1], j)], axis=0) for j, g in probs]
    queries = [jnp.concatenate([q[j * WINDOW:(j + 1) * WINDOW, (2 * g + pair) * WINDOW:(2 * g + pair + 1) * WINDOW]
                                for pair in range(2)], axis=0).astype(BF16) for j, g in probs]
    s = [_dot_nt(k, x) * (HEAD_DIM ** -0.5) + bias_ref[g] for k, x, (_, g) in zip(keys, queries, probs)]
    s = [x + before_start if j == 0 else x for x, (j, _) in zip(s, probs)]
    outs = {}
    sub = [(i, par) for i in range(len(probs)) for par in range(2)]
    s = [s[i][halves[par]] for i, par in sub]
    m = [jnp.maximum(jnp.max(x, axis=0, keepdims=True), sinks[probs[i][1], par]) for x, (i, par) in zip(s, sub)]
    p = [jnp.exp(x - mx) for x, mx in zip(s, m)]
    denom = [jnp.sum(x, axis=0, keepdims=True) + jnp.exp(sinks[probs[i][1], par] - mx)
             for x, mx, (i, par) in zip(p, m, sub)]
    for x, d, (i, par) in zip(p, denom, sub):
        j, g = probs[i]
        v_t = vt_ext[g * HEAD_DIM:(g + 1) * HEAD_DIM, j * WINDOW:(j + 2) * WINDOW]
        outs[j, g, par] = _dot(v_t, x.astype(BF16)) / d
    blocks = [jnp.concatenate([outs[j, g, par][:, pair * WINDOW:(pair + 1) * WINDOW]
                               for g in range(N_KV_HEADS) for pair in range(2) for par in range(2)], axis=0)
              for j in range(n_blocks)]
    return jnp.concatenate(blocks, axis=1)


def _pool_block(p_ext, pos0, pool_w_ref, pool_scale):
    tile = p_ext.shape[0] - POOL_HALO
    count = (lax.broadcasted_iota(jnp.int32, (tile, 1), 0) + pos0 + 1).astype(F32)
    outs = []
    for gi, w in enumerate(POOL_WINDOWS):
        pe = p_ext[:, gi * POOL_GROUP:(gi + 1) * POOL_GROUP]
        acc, span = pe, 1
        while span < w:
            acc = acc[span:] + acc[:-span]
            span *= 2
        win = acc[acc.shape[0] - tile:]
        pg = pe[POOL_HALO:]
        mean = win / jnp.minimum(count, float(w))
        outs.append(_dot((mean - pg).astype(BF16), pool_w_ref[gi]))
    return jnp.concatenate(outs, axis=1) * pool_scale


def _mixer_ab_body(x_ref, gain_ref, w_in_ref, qn_ref, kn_ref, sinks_ref, pool_w_ref, pool_scale_ref, w_out_ref,
                   o_ref, kz_ext, vt_ext, p_ext, bias_s):
    t = pl.program_id(1)
    tile = x_ref.shape[0]

    @pl.when((pl.program_id(0) == 0) & (t == 0))
    def _():
        for g in range(N_KV_HEADS):
            bias_s[g] = _alibi_bias(g)

    @pl.when(t == 0)
    def _():
        kz_ext[:, 0:WINDOW, :] = jnp.zeros((2 * N_KV_HEADS, WINDOW, KV_WIDTH), BF16)
        vt_ext[:, 0:WINDOW] = jnp.zeros((KV_WIDTH, WINDOW), BF16)
        p_ext[0:POOL_HALO, :] = jnp.zeros((POOL_HALO, POOL_WIDTH), F32)

    x = x_ref[...]
    h = _rms_norm(x, gain_ref[...]).astype(BF16)
    z = _dot(h, w_in_ref[...])
    q = z[:, :ATTN_WIDTH]
    k = z[:, ATTN_WIDTH:ATTN_WIDTH + KV_WIDTH]
    v = z[:, ATTN_WIDTH + KV_WIDTH:ATTN_WIDTH + 2 * KV_WIDTH]
    p = z[:, ATTN_WIDTH + 2 * KV_WIDTH:]

    hs = _head_sum_matrix(MXU_TILE, 1.0 / HEAD_DIM)
    q = q * lax.rsqrt(_head_sums(q * q, hs) + RMS_EPS) * qn_ref[...]
    k = k * lax.rsqrt(_dot((k * k).astype(BF16), hs[:KV_WIDTH, :KV_WIDTH]) + RMS_EPS) * kn_ref[...]

    k_swapped = pltpu.roll(k, HEAD_DIM, axis=1)
    low_half = lax.broadcasted_iota(jnp.int32, k.shape, 1) < HEAD_DIM
    for i, (src, keep_low) in enumerate([(k, True), (k_swapped, False), (k_swapped, True), (k, False)]):
        kz_ext[i, WINDOW:, :] = jnp.where(low_half == keep_low, src, 0.0).astype(BF16)
    vt_ext[:, WINDOW:] = v.T.astype(BF16)
    p_ext[POOL_HALO:, :] = p

    o_attn_t = _attention(q, [kz_ext.at[i] for i in range(2 * N_KV_HEADS)], vt_ext, bias_s, sinks_ref,
                          jnp.where(t == 0, WINDOW, 0))
    o_pool = _pool_block(p_ext[...], t * tile, pool_w_ref, pool_scale_ref[...])

    mixed = jnp.concatenate([o_attn_t.T, o_pool], axis=1).astype(BF16)
    o_ref[...] = x + _dot(mixed, w_out_ref[...])

    kz_ext[:, 0:WINDOW, :] = kz_ext[:, tile:tile + WINDOW, :]
    vt_ext[:, 0:WINDOW] = vt_ext[:, tile:tile + WINDOW]
    p_ext[0:POOL_HALO, :] = p_ext[tile:tile + POOL_HALO, :]


def _mixer_ab_call(x, gain, w_in, q_gain, k_gain, sinks, pool_w, pool_scale, w_out):
    b, t, _ = x.shape
    tile = min(AB_TILE, t)
    full = lambda *shape: pl.BlockSpec(shape, lambda i, j: (0,) * len(shape))
    return pl.pallas_call(
        _mixer_ab_body,
        grid=(b, t // tile),
        in_specs=[
            pl.BlockSpec((None, tile, D_MODEL), lambda i, j: (i, j, 0)),
            full(1, D_MODEL),
            full(D_MODEL, AB_IN_WIDTH),
            full(1, ATTN_WIDTH),
            full(1, KV_WIDTH),
            pl.BlockSpec(memory_space=pltpu.SMEM),
            full(len(POOL_WINDOWS), POOL_GROUP, POOL_GROUP),
            full(1, POOL_WIDTH),
            full(ATTN_WIDTH + POOL_WIDTH, D_MODEL),
        ],
        out_specs=pl.BlockSpec((None, tile, D_MODEL), lambda i, j: (i, j, 0)),
        out_shape=jax.ShapeDtypeStruct(x.shape, F32),
        scratch_shapes=[
            pltpu.VMEM((2 * N_KV_HEADS, WINDOW + tile, KV_WIDTH), BF16),
            pltpu.VMEM((KV_WIDTH, WINDOW + tile), BF16),
            pltpu.VMEM((POOL_HALO + tile, POOL_WIDTH), F32),
            pltpu.VMEM((N_KV_HEADS, 2 * 2 * WINDOW, 2 * WINDOW), F32),
        ],
        compiler_params=pltpu.CompilerParams(
            dimension_semantics=("arbitrary", "arbitrary"), vmem_limit_bytes=VMEM_LIMIT_BYTES),
        name="mixer_ab",
    )(x, gain, w_in, q_gain, k_gain, sinks, pool_w, pool_scale, w_out)


def _stack_heads(x):
    xb = x.astype(BF16)
    lane = lax.broadcasted_iota(jnp.int32, (WKV_CHUNK, 2 * RWKV_HEAD), 1)
    zeros = jnp.zeros((WKV_CHUNK, 2 * RWKV_HEAD), BF16)
    blocks = []
    for h in range(HEADS_PER_QUAD):
        pair = xb[:, (h // 2) * 2 * RWKV_HEAD:(h // 2 + 1) * 2 * RWKV_HEAD]
        keep = jnp.where((lane >= RWKV_HEAD) == (h % 2 == 1), pair, jnp.zeros((), BF16))
        blocks.append(jnp.concatenate([keep, zeros] if h < 2 else [zeros, keep], axis=1))
    return jnp.concatenate(blocks, axis=0)


INV_BASE = 8


def _inverse_masks():
    row = lax.broadcasted_iota(jnp.int32, (WKV_CHUNK, MXU_TILE), 0)
    col = lax.broadcasted_iota(jnp.int32, (WKV_CHUNK, MXU_TILE), 1) & (WKV_CHUNK - 1)
    same_block = lambda size: (row // size) == (col // size)
    levels, size = [], INV_BASE
    while size < WKV_CHUNK:
        levels.append(same_block(2 * size) & jnp.logical_not(same_block(size)))
        size *= 2
    return row == col, same_block(INV_BASE), tuple(levels)


def _unit_lower_inverse(ns, stack, inv_masks):
    eye, base_blocks, levels = inv_masks
    mm = lambda lhs, rhs: [_dot(a.astype(BF16), stack(b)) for a, b in zip(lhs, rhs)]
    add = lambda xs, ys: [x + y for x, y in zip(xs, ys)]
    nd = [jnp.where(base_blocks, n, 0.0) for n in ns]
    inv = [jnp.where(eye, 1.0, 0.0) + x for x in nd]
    power, span = nd, 1
    while 2 * span < INV_BASE:
        power = mm(power, power)
        inv = add(inv, mm(inv, power))
        span *= 2
    for off_blocks in levels:
        n_off = [jnp.where(off_blocks, n, 0.0) for n in ns]
        inv = add(inv, mm(inv, mm(n_off, inv)))
    return inv


def _wkv_operands(lw, r, k, v, kk, a, prefix):
    lw_hi = lw.astype(BF16)
    lw_mid = (lw - lw_hi.astype(F32)).astype(BF16)
    lw_lo = (lw - lw_hi.astype(F32) - lw_mid.astype(F32)).astype(BF16)
    cum = _dot(prefix, lw_hi) + _dot(prefix, lw_mid) + _dot(prefix, lw_lo)
    total = cum[WKV_CHUNK - 1:WKV_CHUNK, :]
    d_inv = jnp.exp(-cum)
    d_end = jnp.exp(total - cum)
    b = kk * a
    ops = dict(
        r_t=r * jnp.exp(cum), a_t=-kk * jnp.exp(cum - lw), k_t=k * d_inv, b_t=b * d_inv,
        k_end=k * d_end, b_end=b * d_end, v=v)
    return {name: x.astype(BF16) for name, x in ops.items()}, jnp.exp(total)


def _wkv_local(probs, masks):
    block_diag, strict, incl, inv_masks = masks
    stack = _stack_heads
    lhs = [jnp.concatenate([p["a_t"], p["r_t"]], axis=0) for p in probs]
    with_k = [_dot_nt(l, stack(p["k_t"])) for l, p in zip(lhs, probs)]
    with_b = [_dot_nt(l, stack(p["b_t"])) for l, p in zip(lhs, probs)]
    a_kv = [jnp.concatenate([jnp.where(strict, x[:WKV_CHUNK], 0.0), jnp.where(incl, x[WKV_CHUNK:], 0.0)],
                            axis=0).astype(BF16) for x in with_k]
    a_ab = [jnp.where(strict, x[:WKV_CHUNK], 0.0) for x in with_b]
    a_rb = [jnp.where(incl, x[WKV_CHUNK:], 0.0).astype(BF16) for x in with_b]
    times_v = [_dot(a, stack(p["v"])) for a, p in zip(a_kv, probs)]
    inv = [x.astype(BF16) for x in _unit_lower_inverse(a_ab, stack, inv_masks)]
    u_local = [_dot(i, stack(tv[:WKV_CHUNK])) for i, tv in zip(inv, times_v)]
    w = [_dot(i, stack(p["a_t"])) for i, p in zip(inv, probs)]
    return [dict(wr=jnp.concatenate([wi.astype(BF16), p["r_t"]], axis=0), u=ul, y=tv[WKV_CHUNK:], a_rb=ar)
            for wi, p, ul, tv, ar in zip(w, probs, u_local, times_v, a_rb)]


def _wkv_carry(probs, local, decay_end, states, masks):
    block_diag = masks[0]
    stack = _stack_heads
    from_state = [_dot_nt(l["wr"], s.astype(BF16)) for l, s in zip(local, states)]
    u = [f[:WKV_CHUNK] + l["u"] for f, l in zip(from_state, local)]
    via_u = [_dot(l["a_rb"], stack(x)) for l, x in zip(local, u)]
    outer = [_dot_tn(jnp.concatenate([p["v"], x.astype(BF16)], axis=0),
                     jnp.concatenate([p["k_end"], p["b_end"]], axis=0)) for p, x in zip(probs, u)]
    y = [f[WKV_CHUNK:] + l["y"] + z for f, l, z in zip(from_state, local, via_u)]
    new_states = [s * d + jnp.where(block_diag, o, 0.0) for s, d, o in zip(states, decay_end, outer)]
    return y, new_states


def _rwkv_body(x_ref, gain_ref, mu_ref, wr_ref, wk_ref, wv_ref, w0_ref, w1_ref, w2_ref, a0_ref, a1_ref, a2_ref,
               g1_ref, g2_ref, kk_ref, ka_ref, rk_ref, lnw_ref, lnb_ref, wo_ref, o_ref,
               hprev, state, r_s, k_s, v_s, lw_s, kk_s, a_s, y_s):
    t = pl.program_id(1)
    tile = x_ref.shape[0]

    @pl.when(t == 0)
    def _():
        hprev[...] = jnp.zeros(hprev.shape, F32)
        state[...] = jnp.zeros(state.shape, F32)

    x = x_ref[...]
    h = _rms_norm(x, gain_ref[...])
    row = lax.broadcasted_iota(jnp.int32, (tile, 1), 0)
    shifted = jnp.where(row == 0, hprev[0:1, :], pltpu.roll(h, 1, axis=0))
    hprev[0:1, :] = h[tile - 1:tile, :]
    xx = shifted - h
    mix = lambda i: (h + xx * mu_ref[i:i + 1, :]).astype(BF16)
    r = _dot(mix(0), wr_ref[...])
    k = _dot(mix(2), wk_ref[...])
    v = _dot(mix(3), wv_ref[...])
    w_log = -jax.nn.softplus(-(w0_ref[...] + _dot(jnp.tanh(_dot(mix(1), w1_ref[...])).astype(BF16), w2_ref[...]))) - 0.5
    a = jax.nn.sigmoid(a0_ref[...] + _dot(_dot(mix(4), a1_ref[...]).astype(BF16), a2_ref[...]))
    g = _dot(jax.nn.sigmoid(_dot(mix(5), g1_ref[...])).astype(BF16), g2_ref[...])

    hs = _head_sum_matrix(MXU_TILE, 1.0)
    kk = k * kk_ref[...]
    kk = kk * lax.rsqrt(jnp.maximum(_head_sums(kk * kk, hs), 1e-24))
    k = k * (1.0 + (a - 1.0) * ka_ref[...])
    r_s[...] = r
    k_s[...] = k
    v_s[...] = v
    lw_s[...] = -jnp.exp(w_log)
    kk_s[...] = kk
    a_s[...] = a

    rr = lax.broadcasted_iota(jnp.int32, (MXU_TILE, MXU_TILE), 0) >> 6
    cc = lax.broadcasted_iota(jnp.int32, (MXU_TILE, MXU_TILE), 1) >> 6
    block_diag = rr == cc
    ti = lax.broadcasted_iota(jnp.int32, (WKV_CHUNK, MXU_TILE), 0)
    tj = lax.broadcasted_iota(jnp.int32, (WKV_CHUNK, MXU_TILE), 1) & (WKV_CHUNK - 1)
    masks = (block_diag, tj < ti, tj <= ti, _inverse_masks())
    ci = lax.broadcasted_iota(jnp.int32, (WKV_CHUNK, WKV_CHUNK), 0)
    cj = lax.broadcasted_iota(jnp.int32, (WKV_CHUNK, WKV_CHUNK), 1)
    prefix = jnp.where(cj <= ci, 1.0, 0.0).astype(BF16)

    n_chunks = tile // WKV_CHUNK
    quads = [slice(q * MXU_TILE, (q + 1) * MXU_TILE) for q in range(N_QUADS)]
    chunk_rows = [slice(c * WKV_CHUNK, (c + 1) * WKV_CHUNK) for c in range(n_chunks)]
    operands = [_wkv_operands(lw_s[rows, :], r_s[rows, :], k_s[rows, :], v_s[rows, :], kk_s[rows, :], a_s[rows, :],
                              prefix) for rows in chunk_rows]
    probs = [{name: x[:, lanes] for name, x in ops.items()} for ops, _ in operands for lanes in quads]
    local = _wkv_local(probs, masks)
    states = [state[q] for q in range(N_QUADS)]
    for c, rows in enumerate(chunk_rows):
        group = slice(c * N_QUADS, (c + 1) * N_QUADS)
        decay_end = [operands[c][1][:, lanes] for lanes in quads]
        ys, states = _wkv_carry(probs[group], local[group], decay_end, states, masks)
        for lanes, y in zip(quads, ys):
            y_s[rows, lanes] = y
    for q in range(N_QUADS):
        state[q] = states[q]

    y = y_s[...]
    r = r_s[...]
    k = k_s[...]
    v = v_s[...]
    mean = _head_sums(y, hs) * (1.0 / RWKV_HEAD)
    cen = y - mean
    var = _head_sums(cen * cen, hs) * (1.0 / RWKV_HEAD)
    yn = cen * lax.rsqrt(var + GN_EPS) * lnw_ref[...] + lnb_ref[...]
    bonus = _head_sums(r * k * rk_ref[...], hs) * v
    out = ((yn + bonus) * g).astype(BF16)
    o_ref[...] = x + _dot(out, wo_ref[...])


def _rwkv_call(x, gain, mu, wr, wk, wv, w0, w1, w2, a0, a1, a2, g1, g2, k_k, k_a, r_k, lnw, lnb, wo):
    b, t, _ = x.shape
    tile = min(RWKV_TILE, t)
    full = lambda *shape: pl.BlockSpec(shape, lambda i, j: (0,) * len(shape))
    vec = full(1, D_MODEL)
    sq = full(D_MODEL, D_MODEL)
    act = lambda: pltpu.VMEM((tile, D_MODEL), F32)
    return pl.pallas_call(
        _rwkv_body,
        grid=(b, t // tile),
        in_specs=[
            pl.BlockSpec((None, tile, D_MODEL), lambda i, j: (i, j, 0)),
            vec, full(mu.shape[0], D_MODEL), sq, sq, sq,
            vec, full(*w1.shape), full(*w2.shape),
            vec, full(*a1.shape), full(*a2.shape),
            full(*g1.shape), full(*g2.shape),
            vec, vec, vec, vec, vec, sq,
        ],
        out_specs=pl.BlockSpec((None, tile, D_MODEL), lambda i, j: (i, j, 0)),
        out_shape=jax.ShapeDtypeStruct(x.shape, F32),
        scratch_shapes=[
            pltpu.VMEM((8, D_MODEL), F32),
            pltpu.VMEM((N_QUADS, MXU_TILE, MXU_TILE), F32),
            act(), act(), act(), act(), act(), act(), act(),
        ],
        compiler_params=pltpu.CompilerParams(
            dimension_semantics=("arbitrary", "arbitrary"), vmem_limit_bytes=VMEM_LIMIT_BYTES),
        name="rwkv7",
    )(x, gain, mu, wr, wk, wv, w0, w1, w2, a0, a1, a2, g1, g2, k_k, k_a, r_k, lnw, lnb, wo)


def kernel(x, ffn_norm, ffn_w_gate, ffn_w_up, ffn_w_down, ab_norm, ab_w_in, q_norm, k_norm, attn_sinks, pool_w, pool_scale, ab_w_out, c_norm, c_mu, c_w_r, c_w_k, c_w_v, c_w0, c_w1, c_w2, c_a0, c_a1, c_a2, c_g1, c_g2, c_k_k, c_k_a, c_r_k, c_lnx_w, c_lnx_b, c_w_o):
    b, t, d = x.shape
    depth = ffn_norm.shape[0]
    bf = lambda w: w.astype(BF16)
    row = lambda p: p.reshape(1, -1).astype(F32)
    ffn_gain = ffn_norm.reshape(depth, 2, 1, d)
    wg, wu, wd = bf(ffn_w_gate), bf(ffn_w_up), bf(ffn_w_down)

    def ffn(x, layer, half):
        return _ffn_call(x.reshape(b * t, d), ffn_gain, wg, wu, wd, layer, half).reshape(b, t, d)

    for layer in range(depth):
        x = ffn(x, layer, 0)
        j = layer // 2
        if layer % 2 == 0:
            x = _mixer_ab_call(
                x, row(ab_norm[j]), bf(ab_w_in[j]), row(jnp.tile(q_norm[j], N_Q_HEADS)),
                row(jnp.tile(k_norm[j], N_KV_HEADS)), attn_sinks[j].astype(F32), bf(pool_w[j]),
                row(pool_scale[j]), bf(ab_w_out[j]))
        else:
            x = _rwkv_call(
                x, row(c_norm[j]), c_mu[j].astype(F32), bf(c_w_r[j]), bf(c_w_k[j]), bf(c_w_v[j]),
                row(c_w0[j]), bf(c_w1[j]), bf(c_w2[j]), row(c_a0[j]), bf(c_a1[j]), bf(c_a2[j]),
                bf(c_g1[j]), bf(c_g2[j]), row(c_k_k[j]), row(c_k_a[j]), row(c_r_k[j]),
                row(c_lnx_w[j]), row(c_lnx_b[j]), bf(c_w_o[j]))
        x = ffn(x, layer, 1)
    return x
```

```python
import jax
import jax.numpy as jnp
from jax import lax
from jax.experimental import pallas as pl
from jax.experimental.pallas import tpu as pltpu

F32 = jnp.float32
BF16 = jnp.bfloat16

D_MODEL = 1024
D_FF = 2816
HEAD_DIM = 64
N_Q_HEADS = 8
N_KV_HEADS = 2
GQA_GROUPS = N_Q_HEADS // N_KV_HEADS
WINDOW = 128
ATTN_WIDTH = N_Q_HEADS * HEAD_DIM
KV_WIDTH = N_KV_HEADS * HEAD_DIM
POOL_WINDOWS = (2, 4, 8, 16)
POOL_GROUP = 128
POOL_WIDTH = len(POOL_WINDOWS) * POOL_GROUP
AB_IN_WIDTH = ATTN_WIDTH + 2 * KV_WIDTH + POOL_WIDTH
POOL_HALO = 16
RWKV_HEAD = 64
RMS_EPS = 1e-6
GN_EPS = 64e-5

MXU_TILE = 256
HEADS_PER_QUAD = MXU_TILE // RWKV_HEAD
N_QUADS = D_MODEL // MXU_TILE
WKV_CHUNK = 64
VMEM_LIMIT_BYTES = 56 * 1024 * 1024

FFN_TILE = 512
FFN_F_CHUNK = 256
AB_TILE = 512
RWKV_TILE = 256


def _dot(a, b):
    return jnp.dot(a, b, preferred_element_type=F32)


def _dot_nt(a, b):
    return lax.dot_general(a, b, (((1,), (1,)), ((), ())), preferred_element_type=F32)


def _dot_tn(a, b):
    return lax.dot_general(a, b, (((0,), (0,)), ((), ())), preferred_element_type=F32)


def _rms_norm(x, gain):
    return x * lax.rsqrt(jnp.mean(x * x, axis=-1, keepdims=True) + RMS_EPS) * gain


def _head_sum_matrix(width, scale):
    r = lax.broadcasted_iota(jnp.int32, (width, width), 0) >> 6
    c = lax.broadcasted_iota(jnp.int32, (width, width), 1) >> 6
    return jnp.where(r == c, scale, 0.0).astype(BF16)


def _head_sums(z, hs):
    parts = [_dot(z[:, i:i + MXU_TILE].astype(BF16), hs) for i in range(0, z.shape[1], MXU_TILE)]
    return parts[0] if len(parts) == 1 else jnp.concatenate(parts, axis=1)


def _ffn_body(x_ref, gain_ref, wg_ref, wu_ref, wd_ref, o_ref):
    x = x_ref[...]
    h = _rms_norm(x, gain_ref[...]).astype(BF16)
    acc = jnp.zeros(x.shape, F32)
    for c in range(0, D_FF, FFN_F_CHUNK):
        gate = _dot(h, wg_ref[:, c:c + FFN_F_CHUNK])
        up = _dot(h, wu_ref[:, c:c + FFN_F_CHUNK])
        act = (jax.nn.silu(gate) * up).astype(BF16)
        acc = acc + _dot(act, wd_ref[c:c + FFN_F_CHUNK, :])
    o_ref[...] = x + 0.5 * acc


def _ffn_call(x2, gain, wg, wu, wd, layer, half):
    n = x2.shape[0]
    tile = min(FFN_TILE, n)
    w_idx = lambda i: (layer, half, 0, 0)
    return pl.pallas_call(
        _ffn_body,
        grid=(n // tile,),
        in_specs=[
            pl.BlockSpec((tile, D_MODEL), lambda i: (i, 0)),
            pl.BlockSpec((None, None, 1, D_MODEL), w_idx),
            pl.BlockSpec((None, None, D_MODEL, D_FF), w_idx),
            pl.BlockSpec((None, None, D_MODEL, D_FF), w_idx),
            pl.BlockSpec((None, None, D_FF, D_MODEL), w_idx),
        ],
        out_specs=pl.BlockSpec((tile, D_MODEL), lambda i: (i, 0)),
        out_shape=jax.ShapeDtypeStruct(x2.shape, F32),
        compiler_params=pltpu.CompilerParams(
            dimension_semantics=("parallel",), vmem_limit_bytes=VMEM_LIMIT_BYTES),
        name=f"ffn_l{layer}_h{half}",
    )(x2, gain, wg, wu, wd)


def _alibi_bias(g):
    shape = (2 * 2 * WINDOW, 2 * WINDOW)
    row = lax.broadcasted_iota(jnp.int32, shape, 0)
    lane = lax.broadcasted_iota(jnp.int32, shape, 1)
    dist = (lane & (WINDOW - 1)) - (row & (2 * WINDOW - 1)) + WINDOW
    slope = (2.0 ** -(GQA_GROUPS * g + 1)) * jnp.where(lane >= WINDOW, 0.25, 1.0) * jnp.where(
        row >= 2 * WINDOW, 0.5, 1.0)
    return jnp.where((dist >= 0) & (dist < WINDOW), -slope * dist.astype(F32), -jnp.inf)


def _attention(q, kz, vt_ext, bias_ref, sinks_ref, first_key):
    n_blocks = q.shape[0] // WINDOW
    probs = [(j, g) for j in range(n_blocks) for g in range(N_KV_HEADS)]
    halves = [slice(0, 2 * WINDOW), slice(2 * WINDOW, 4 * WINDOW)]
    lane = lax.broadcasted_iota(jnp.int32, (1, 2 * WINDOW), 1)
    sinks = {(g, par): jnp.where(lane < WINDOW, sinks_ref[GQA_GROUPS * g + par], sinks_ref[GQA_GROUPS * g + 2 + par])
             for g in range(N_KV_HEADS) for par in range(2)}
    key_row = lax.broadcasted_iota(jnp.int32, (2 * 2 * WINDOW, 2 * WINDOW), 0) & (2 * WINDOW - 1)
    before_start = jnp.where(key_row >= first_key, 0.0, -jnp.inf)

    def key_window(ref, j):
        return ref[j * WINDOW:(j + 2) * WINDOW, :]

    keys = [jnp.concatenate([key_window(kz[2 * g], j), key_window(kz[2 * g + 1], j)], axis=0) for j, g in probs]
    queries = [jnp.concatenate([q[j * WINDOW:(j + 1) * WINDOW, (2 * g + pair) * WINDOW:(2 * g + pair + 1) * WINDOW]
                                for pair in range(2)], axis=0).astype(BF16) for j, g in probs]
    s = [_dot_nt(k, x) * (HEAD_DIM ** -0.5) + bias_ref[g] for k, x, (_, g) in zip(keys, queries, probs)]
    s = [x + before_start if j == 0 else x for x, (j, _) in zip(s, probs)]
    outs = {}
    sub = [(i, par) for i in range(len(probs)) for par in range(2)]
    s = [s[i][halves[par]] for i, par in sub]
    m = [jnp.maximum(jnp.max(x, axis=0, keepdims=True), sinks[probs[i][1], par]) for x, (i, par) in zip(s, sub)]
    p = [jnp.exp(x - mx) for x, mx in zip(s, m)]
    denom = [jnp.sum(x, axis=0, keepdims=True) + jnp.exp(sinks[probs[i][1], par] - mx)
             for x, mx, (i, par) in zip(p, m, sub)]
    for x, d, (i, par) in zip(p, denom, sub):
        j, g = probs[i]
        v_t = vt_ext[g * HEAD_DIM:(g + 1) * HEAD_DIM, j * WINDOW:(j + 2) * WINDOW]
        outs[j, g, par] = _dot(v_t, x.astype(BF16)) / d
    blocks = [jnp.concatenate([outs[j, g, par][:, pair * WINDOW:(pair + 1) * WINDOW]
                               for g in range(N_KV_HEADS) for pair in range(2) for par in range(2)], axis=0)
              for j in range(n_blocks)]
    return jnp.concatenate(blocks, axis=1)


def _pool_block(p_ext, pos0, pool_w_ref, pool_scale):
    tile = p_ext.shape[0] - POOL_HALO
    count = (lax.broadcasted_iota(jnp.int32, (tile, 1), 0) + pos0 + 1).astype(F32)
    outs = []
    for gi, w in enumerate(POOL_WINDOWS):
        pe = p_ext[:, gi * POOL_GROUP:(gi + 1) * POOL_GROUP]
        acc, span = pe, 1
        while span < w:
            acc = acc[span:] + acc[:-span]
            span *= 2
        win = acc[acc.shape[0] - tile:]
        pg = pe[POOL_HALO:]
        mean = win / jnp.minimum(count, float(w))
        outs.append(_dot((mean - pg).astype(BF16), pool_w_ref[gi]))
    return jnp.concatenate(outs, axis=1) * pool_scale


def _mixer_ab_body(x_ref, gain_ref, w_in_ref, qn_ref, kn_ref, sinks_ref, pool_w_ref, pool_scale_ref, w_out_ref,
                   o_ref, kz_ext, vt_ext, p_ext, bias_s):
    t = pl.program_id(1)
    tile = x_ref.shape[0]

    @pl.when((pl.program_id(0) == 0) & (t == 0))
    def _():
        for g in range(N_KV_HEADS):
            bias_s[g] = _alibi_bias(g)

    @pl.when(t == 0)
    def _():
        kz_ext[:, 0:WINDOW, :] = jnp.zeros((2 * N_KV_HEADS, WINDOW, KV_WIDTH), BF16)
        vt_ext[:, 0:WINDOW] = jnp.zeros((KV_WIDTH, WINDOW), BF16)
        p_ext[0:POOL_HALO, :] = jnp.zeros((POOL_HALO, POOL_WIDTH), F32)

    x = x_ref[...]
    h = _rms_norm(x, gain_ref[...]).astype(BF16)
    z = _dot(h, w_in_ref[...])
    q = z[:, :ATTN_WIDTH]
    k = z[:, ATTN_WIDTH:ATTN_WIDTH + KV_WIDTH]
    v = z[:, ATTN_WIDTH + KV_WIDTH:ATTN_WIDTH + 2 * KV_WIDTH]
    p = z[:, ATTN_WIDTH + 2 * KV_WIDTH:]

    hs = _head_sum_matrix(MXU_TILE, 1.0 / HEAD_DIM)
    q = q * lax.rsqrt(_head_sums(q * q, hs) + RMS_EPS) * qn_ref[...]
    k = k * lax.rsqrt(_dot((k * k).astype(BF16), hs[:KV_WIDTH, :KV_WIDTH]) + RMS_EPS) * kn_ref[...]

    k_swapped = pltpu.roll(k, HEAD_DIM, axis=1)
    low_half = lax.broadcasted_iota(jnp.int32, k.shape, 1) < HEAD_DIM
    for i, (src, keep_low) in enumerate([(k, True), (k_swapped, False), (k_swapped, True), (k, False)]):
        kz_ext[i, WINDOW:, :] = jnp.where(low_half == keep_low, src, 0.0).astype(BF16)
    vt_ext[:, WINDOW:] = v.T.astype(BF16)
    p_ext[POOL_HALO:, :] = p

    o_attn_t = _attention(q, [kz_ext.at[i] for i in range(2 * N_KV_HEADS)], vt_ext, bias_s, sinks_ref,
                          jnp.where(t == 0, WINDOW, 0))
    o_pool = _pool_block(p_ext[...], t * tile, pool_w_ref, pool_scale_ref[...])

    mixed = jnp.concatenate([o_attn_t.T, o_pool], axis=1).astype(BF16)
    o_ref[...] = x + _dot(mixed, w_out_ref[...])

    kz_ext[:, 0:WINDOW, :] = kz_ext[:, tile:tile + WINDOW, :]
    vt_ext[:, 0:WINDOW] = vt_ext[:, tile:tile + WINDOW]
    p_ext[0:POOL_HALO, :] = p_ext[tile:tile + POOL_HALO, :]


def _mixer_ab_call(x, gain, w_in, q_gain, k_gain, sinks, pool_w, pool_scale, w_out):
    b, t, _ = x.shape
    tile = min(AB_TILE, t)
    full = lambda *shape: pl.BlockSpec(shape, lambda i, j: (0,) * len(shape))
    return pl.pallas_call(
        _mixer_ab_body,
        grid=(b, t // tile),
        in_specs=[
            pl.BlockSpec((None, tile, D_MODEL), lambda i, j: (i, j, 0)),
            full(1, D_MODEL),
            full(D_MODEL, AB_IN_WIDTH),
            full(1, ATTN_WIDTH),
            full(1, KV_WIDTH),
            pl.BlockSpec(memory_space=pltpu.SMEM),
            full(len(POOL_WINDOWS), POOL_GROUP, POOL_GROUP),
            full(1, POOL_WIDTH),
            full(ATTN_WIDTH + POOL_WIDTH, D_MODEL),
        ],
        out_specs=pl.BlockSpec((None, tile, D_MODEL), lambda i, j: (i, j, 0)),
        out_shape=jax.ShapeDtypeStruct(x.shape, F32),
        scratch_shapes=[
            pltpu.VMEM((2 * N_KV_HEADS, WINDOW + tile, KV_WIDTH), BF16),
            pltpu.VMEM((KV_WIDTH, WINDOW + tile), BF16),
            pltpu.VMEM((POOL_HALO + tile, POOL_WIDTH), F32),
            pltpu.VMEM((N_KV_HEADS, 2 * 2 * WINDOW, 2 * WINDOW), F32),
        ],
        compiler_params=pltpu.CompilerParams(
            dimension_semantics=("arbitrary", "arbitrary"), vmem_limit_bytes=VMEM_LIMIT_BYTES),
        name="mixer_ab",
    )(x, gain, w_in, q_gain, k_gain, sinks, pool_w, pool_scale, w_out)


def _stack_heads(x):
    xb = x.astype(BF16)
    lane = lax.broadcasted_iota(jnp.int32, (WKV_CHUNK, 2 * RWKV_HEAD), 1)
    zeros = jnp.zeros((WKV_CHUNK, 2 * RWKV_HEAD), BF16)
    blocks = []
    for h in range(HEADS_PER_QUAD):
        pair = xb[:, (h // 2) * 2 * RWKV_HEAD:(h // 2 + 1) * 2 * RWKV_HEAD]
        keep = jnp.where((lane >= RWKV_HEAD) == (h % 2 == 1), pair, jnp.zeros((), BF16))
        blocks.append(jnp.concatenate([keep, zeros] if h < 2 else [zeros, keep], axis=1))
    return jnp.concatenate(blocks, axis=0)


def _stack_heads_transposed(x):
    pair_width = 2 * RWKV_HEAD
    row = lax.broadcasted_iota(jnp.int32, (pair_width, pair_width), 0)
    lane = lax.broadcasted_iota(jnp.int32, (pair_width, pair_width), 1)
    same_head = (row >= RWKV_HEAD) == (lane >= RWKV_HEAD)
    zeros = jnp.zeros((pair_width, pair_width), BF16)
    blocks = []
    for pair in range(HEADS_PER_QUAD // 2):
        slab = x[:, pair * pair_width:(pair + 1) * pair_width]
        slab_t = jnp.concatenate([slab, slab], axis=0).T.astype(BF16)
        keep = jnp.where(same_head, slab_t, jnp.zeros((), BF16))
        blocks.append(jnp.concatenate([keep, zeros] if pair == 0 else [zeros, keep], axis=1))
    return jnp.concatenate(blocks, axis=0)


INV_BASE = 8
WKV_STAGGER = 10


def _inverse_masks():
    row = lax.broadcasted_iota(jnp.int32, (WKV_CHUNK, MXU_TILE), 0)
    col = lax.broadcasted_iota(jnp.int32, (WKV_CHUNK, MXU_TILE), 1) & (WKV_CHUNK - 1)
    same_block = lambda size: (row // size) == (col // size)
    levels, size = [], INV_BASE
    while size < WKV_CHUNK:
        levels.append(same_block(2 * size) & jnp.logical_not(same_block(size)))
        size *= 2
    return row == col, same_block(INV_BASE), tuple(levels)


def _unit_lower_inverse(ns, stack, inv_masks):
    eye, base_blocks, levels = inv_masks
    zero = jnp.zeros((), BF16)
    mm = lambda lhs, rhs: [_dot(a, stack(b)).astype(BF16) for a, b in zip(lhs, rhs)]
    add = lambda xs, ys: [x + y for x, y in zip(xs, ys)]
    nd = [jnp.where(base_blocks, n, zero) for n in ns]
    inv = [jnp.where(eye, jnp.ones((), BF16), x) for x in nd]
    power, span = nd, 1
    while 2 * span < INV_BASE:
        power = mm(power, power)
        yield
        inv = add(inv, mm(inv, power))
        yield
        span *= 2
    for off_blocks in levels:
        n_off = [jnp.where(off_blocks, n, zero) for n in ns]
        right = mm(n_off, inv)
        yield
        inv = add(inv, mm(inv, right))
        yield
    return inv


def _wkv_operands(lw, r, k, v, kk, a, prefix):
    lw_hi = lw.astype(BF16)
    lw_mid = (lw - lw_hi.astype(F32)).astype(BF16)
    lw_lo = (lw - lw_hi.astype(F32) - lw_mid.astype(F32)).astype(BF16)
    cum = _dot(prefix, lw_hi) + _dot(prefix, lw_mid) + _dot(prefix, lw_lo)
    total = cum[WKV_CHUNK - 1:WKV_CHUNK, :]
    d_inv = jnp.exp(-cum)
    d_end = jnp.exp(total - cum)
    b = kk * a
    ops = dict(
        r_t=r * jnp.exp(cum), a_t=-kk * jnp.exp(cum - lw), k_t=k * d_inv, b_t=b * d_inv,
        k_end=k * d_end, b_end=b * d_end, v=v)
    return {name: x if name in ("k_t", "b_t") else x.astype(BF16) for name, x in ops.items()}, jnp.exp(total)


_DONE = object()


def _interleave(*gens):
    live = list(gens)
    while live:
        for g in list(live):
            if next(g, _DONE) is _DONE:
                live.remove(g)


def _chain(*gens):
    for g in gens:
        yield from g


def _wkv_local(chunk_inputs, masks, prefix, out):
    _, strict, incl, inv_masks = masks
    stack = _stack_heads
    quads = [slice(q * MXU_TILE, (q + 1) * MXU_TILE) for q in range(N_QUADS)]
    operands = [_wkv_operands(*inputs, prefix) for inputs in chunk_inputs]
    probs = [dict({name: x[:, lanes] for name, x in ops.items()}, decay_end=decay[:, lanes])
             for ops, decay in operands for lanes in quads]
    yield
    lhs = [jnp.concatenate([p["a_t"], p["r_t"]], axis=0) for p in probs]
    with_k = [_dot(l, _stack_heads_transposed(p["k_t"])) for l, p in zip(lhs, probs)]
    yield
    with_b = [_dot(l, _stack_heads_transposed(p["b_t"])) for l, p in zip(lhs, probs)]
    a_kv = [jnp.concatenate([jnp.where(strict, x[:WKV_CHUNK], 0.0), jnp.where(incl, x[WKV_CHUNK:], 0.0)],
                            axis=0).astype(BF16) for x in with_k]
    a_ab = [jnp.where(strict, x[:WKV_CHUNK], 0.0).astype(BF16) for x in with_b]
    a_rb = [jnp.where(incl, x[WKV_CHUNK:], 0.0).astype(BF16) for x in with_b]
    yield
    times_v = [_dot(a, stack(p["v"])) for a, p in zip(a_kv, probs)]
    yield
    inv = yield from _unit_lower_inverse(a_ab, stack, inv_masks)
    u_local = [_dot(i, stack(tv[:WKV_CHUNK])) for i, tv in zip(inv, times_v)]
    yield
    w = [_dot(i, stack(p["a_t"])) for i, p in zip(inv, probs)]
    out.extend(dict(p, wr=jnp.concatenate([wi.astype(BF16), p["r_t"]], axis=0), u=ul, y=tv[WKV_CHUNK:], a_rb=ar)
               for wi, p, ul, tv, ar in zip(w, probs, u_local, times_v, a_rb))
    yield


def _wkv_carry(chunk_rows, local, states, y_ref, masks):
    block_diag = masks[0]
    stack = _stack_heads
    quads = [slice(q * MXU_TILE, (q + 1) * MXU_TILE) for q in range(N_QUADS)]
    for i, rows in enumerate(chunk_rows):
        group = local[i * N_QUADS:(i + 1) * N_QUADS]
        from_state = [_dot_nt(p["wr"], st.astype(BF16)) for p, st in zip(group, states)]
        yield
        u = [f[:WKV_CHUNK] + p["u"] for f, p in zip(from_state, group)]
        via_u = [_dot(p["a_rb"], stack(x)) for p, x in zip(group, u)]
        yield
        outer = [_dot_tn(jnp.concatenate([p["v"], x.astype(BF16)], axis=0),
                         jnp.concatenate([p["k_end"], p["b_end"]], axis=0)) for p, x in zip(group, u)]
        for lanes, f, p, z in zip(quads, from_state, group, via_u):
            y_ref[rows, lanes] = f[WKV_CHUNK:] + p["y"] + z
        states[:] = [st * p["decay_end"] + jnp.where(block_diag, o, 0.0) for st, p, o in zip(states, group, outer)]
        yield


def _rwkv_body(x_ref, gain_ref, mu_ref, wr_ref, wk_ref, wv_ref, w0_ref, w1_ref, w2_ref, a0_ref, a1_ref, a2_ref,
               g1_ref, g2_ref, kk_ref, ka_ref, rk_ref, lnw_ref, lnb_ref, wo_ref, o_ref,
               hprev, state, r_s, k_s, v_s, lw_s, kk_s, a_s, y_s):
    t = pl.program_id(1)
    tile = x_ref.shape[0]

    @pl.when(t == 0)
    def _():
        hprev[...] = jnp.zeros(hprev.shape, F32)
        state[...] = jnp.zeros(state.shape, F32)

    x = x_ref[...]
    h = _rms_norm(x, gain_ref[...])
    row = lax.broadcasted_iota(jnp.int32, (tile, 1), 0)
    shifted = jnp.where(row == 0, hprev[0:1, :], pltpu.roll(h, 1, axis=0))
    hprev[0:1, :] = h[tile - 1:tile, :]
    xx = shifted - h
    mix = lambda i: (h + xx * mu_ref[i:i + 1, :]).astype(BF16)
    r = _dot(mix(0), wr_ref[...])
    k = _dot(mix(2), wk_ref[...])
    v = _dot(mix(3), wv_ref[...])
    w_log = -jax.nn.softplus(-(w0_ref[...] + _dot(jnp.tanh(_dot(mix(1), w1_ref[...])).astype(BF16), w2_ref[...]))) - 0.5
    a = jax.nn.sigmoid(a0_ref[...] + _dot(_dot(mix(4), a1_ref[...]).astype(BF16), a2_ref[...]))
    g = _dot(jax.nn.sigmoid(_dot(mix(5), g1_ref[...])).astype(BF16), g2_ref[...])

    hs = _head_sum_matrix(MXU_TILE, 1.0)
    kk = k * kk_ref[...]
    kk = kk * lax.rsqrt(jnp.maximum(_head_sums(kk * kk, hs), 1e-24))
    k = k * (1.0 + (a - 1.0) * ka_ref[...])
    r_s[...] = r
    k_s[...] = k
    v_s[...] = v
    lw_s[...] = -jnp.exp(w_log)
    kk_s[...] = kk
    a_s[...] = a

    rr = lax.broadcasted_iota(jnp.int32, (MXU_TILE, MXU_TILE), 0) >> 6
    cc = lax.broadcasted_iota(jnp.int32, (MXU_TILE, MXU_TILE), 1) >> 6
    block_diag = rr == cc
    ti = lax.broadcasted_iota(jnp.int32, (WKV_CHUNK, MXU_TILE), 0)
    tj = lax.broadcasted_iota(jnp.int32, (WKV_CHUNK, MXU_TILE), 1) & (WKV_CHUNK - 1)
    masks = (block_diag, tj < ti, tj <= ti, _inverse_masks())
    ci = lax.broadcasted_iota(jnp.int32, (WKV_CHUNK, WKV_CHUNK), 0)
    cj = lax.broadcasted_iota(jnp.int32, (WKV_CHUNK, WKV_CHUNK), 1)
    prefix = jnp.where(cj <= ci, 1.0, 0.0).astype(BF16)

    n_chunks = tile // WKV_CHUNK
    chunk_rows = [slice(c * WKV_CHUNK, (c + 1) * WKV_CHUNK) for c in range(n_chunks)]
    groups = [chunk_rows[:n_chunks // 2], chunk_rows[n_chunks // 2:]]
    inputs = lambda rows: (lw_s[rows, :], r_s[rows, :], k_s[rows, :], v_s[rows, :], kk_s[rows, :], a_s[rows, :])
    local = [[], []]
    first, second = (_wkv_local([inputs(rows) for rows in group], masks, prefix, out)
                     for group, out in zip(groups, local))
    states = [state[q] for q in range(N_QUADS)]
    for _ in range(WKV_STAGGER):
        next(first)
    _interleave(_chain(first, _wkv_carry(groups[0], local[0], states, y_s, masks)), second)
    _interleave(_wkv_carry(groups[1], local[1], states, y_s, masks))
    for q in range(N_QUADS):
        state[q] = states[q]

    y = y_s[...]
    r = r_s[...]
    k = k_s[...]
    v = v_s[...]
    mean = _head_sums(y, hs) * (1.0 / RWKV_HEAD)
    cen = y - mean
    var = _head_sums(cen * cen, hs) * (1.0 / RWKV_HEAD)
    yn = cen * lax.rsqrt(var + GN_EPS) * lnw_ref[...] + lnb_ref[...]
    bonus = _head_sums(r * k * rk_ref[...], hs) * v
    out = ((yn + bonus) * g).astype(BF16)
    o_ref[...] = x + _dot(out, wo_ref[...])


def _rwkv_call(x, gain, mu, wr, wk, wv, w0, w1, w2, a0, a1, a2, g1, g2, k_k, k_a, r_k, lnw, lnb, wo):
    b, t, _ = x.shape
    tile = min(RWKV_TILE, t)
    full = lambda *shape: pl.BlockSpec(shape, lambda i, j: (0,) * len(shape))
    vec = full(1, D_MODEL)
    sq = full(D_MODEL, D_MODEL)
    act = lambda: pltpu.VMEM((tile, D_MODEL), F32)
    return pl.pallas_call(
        _rwkv_body,
        grid=(b, t // tile),
        in_specs=[
            pl.BlockSpec((None, tile, D_MODEL), lambda i, j: (i, j, 0)),
            vec, full(mu.shape[0], D_MODEL), sq, sq, sq,
            vec, full(*w1.shape), full(*w2.shape),
            vec, full(*a1.shape), full(*a2.shape),
            full(*g1.shape), full(*g2.shape),
            vec, vec, vec, vec, vec, sq,
        ],
        out_specs=pl.BlockSpec((None, tile, D_MODEL), lambda i, j: (i, j, 0)),
        out_shape=jax.ShapeDtypeStruct(x.shape, F32),
        scratch_shapes=[
            pltpu.VMEM((8, D_MODEL), F32),
            pltpu.VMEM((N_QUADS, MXU_TILE, MXU_TILE), F32),
            act(), act(), act(), act(), act(), act(), act(),
        ],
        compiler_params=pltpu.CompilerParams(
            dimension_semantics=("arbitrary", "arbitrary"), vmem_limit_bytes=VMEM_LIMIT_BYTES),
        name="rwkv7",
    )(x, gain, mu, wr, wk, wv, w0, w1, w2, a0, a1, a2, g1, g2, k_k, k_a, r_k, lnw, lnb, wo)


def kernel(x, ffn_norm, ffn_w_gate, ffn_w_up, ffn_w_down, ab_norm, ab_w_in, q_norm, k_norm, attn_sinks, pool_w, pool_scale, ab_w_out, c_norm, c_mu, c_w_r, c_w_k, c_w_v, c_w0, c_w1, c_w2, c_a0, c_a1, c_a2, c_g1, c_g2, c_k_k, c_k_a, c_r_k, c_lnx_w, c_lnx_b, c_w_o):
    b, t, d = x.shape
    depth = ffn_norm.shape[0]
    bf = lambda w: w.astype(BF16)
    row = lambda p: p.reshape(1, -1).astype(F32)
    ffn_gain = ffn_norm.reshape(depth, 2, 1, d)
    wg, wu, wd = bf(ffn_w_gate), bf(ffn_w_up), bf(ffn_w_down)

    def ffn(x, layer, half):
        return _ffn_call(x.reshape(b * t, d), ffn_gain, wg, wu, wd, layer, half).reshape(b, t, d)

    for layer in range(depth):
        x = ffn(x, layer, 0)
        j = layer // 2
        if layer % 2 == 0:
            x = _mixer_ab_call(
                x, row(ab_norm[j]), bf(ab_w_in[j]), row(jnp.tile(q_norm[j], N_Q_HEADS)),
                row(jnp.tile(k_norm[j], N_KV_HEADS)), attn_sinks[j].astype(F32), bf(pool_w[j]),
                row(pool_scale[j]), bf(ab_w_out[j]))
        else:
            x = _rwkv_call(
                x, row(c_norm[j]), c_mu[j].astype(F32), bf(c_w_r[j]), bf(c_w_k[j]), bf(c_w_v[j]),
                row(c_w0[j]), bf(c_w1[j]), bf(c_w2[j]), row(c_a0[j]), bf(c_a1[j]), bf(c_a2[j]),
                bf(c_g1[j]), bf(c_g2[j]), row(c_k_k[j]), row(c_k_a[j]), row(c_r_k[j]),
                row(c_lnx_w[j]), row(c_lnx_b[j]), bf(c_w_o[j]))
        x = ffn(x, layer, 1)
    return x
```

```python
import jax
import jax.numpy as jnp
from jax import lax
from jax.experimental import pallas as pl
from jax.experimental.pallas import tpu as pltpu

F32 = jnp.float32
BF16 = jnp.bfloat16

D_MODEL = 1024
D_FF = 2816
HEAD_DIM = 64
N_Q_HEADS = 8
N_KV_HEADS = 2
GQA_GROUPS = N_Q_HEADS // N_KV_HEADS
WINDOW = 128
ATTN_WIDTH = N_Q_HEADS * HEAD_DIM
KV_WIDTH = N_KV_HEADS * HEAD_DIM
POOL_WINDOWS = (2, 4, 8, 16)
POOL_GROUP = 128
POOL_WIDTH = len(POOL_WINDOWS) * POOL_GROUP
AB_IN_WIDTH = ATTN_WIDTH + 2 * KV_WIDTH + POOL_WIDTH
POOL_HALO = 16
RWKV_HEAD = 64
RMS_EPS = 1e-6
GN_EPS = 64e-5

MXU_TILE = 256
HEADS_PER_QUAD = MXU_TILE // RWKV_HEAD
N_QUADS = D_MODEL // MXU_TILE
WKV_CHUNK = 64
VMEM_LIMIT_BYTES = 56 * 1024 * 1024

FFN_TILE = 512
FFN_F_CHUNK = 256
AB_TILE = 512
RWKV_TILE = 256


def _dot(a, b):
    return jnp.dot(a, b, preferred_element_type=F32)


def _dot_nt(a, b):
    return lax.dot_general(a, b, (((1,), (1,)), ((), ())), preferred_element_type=F32)


def _dot_tn(a, b):
    return lax.dot_general(a, b, (((0,), (0,)), ((), ())), preferred_element_type=F32)


def _rms_norm(x, gain):
    return x * lax.rsqrt(jnp.mean(x * x, axis=-1, keepdims=True) + RMS_EPS) * gain


def _head_sum_matrix(width, scale):
    r = lax.broadcasted_iota(jnp.int32, (width, width), 0) >> 6
    c = lax.broadcasted_iota(jnp.int32, (width, width), 1) >> 6
    return jnp.where(r == c, scale, 0.0).astype(BF16)


def _head_sums(z, hs):
    parts = [_dot(z[:, i:i + MXU_TILE].astype(BF16), hs) for i in range(0, z.shape[1], MXU_TILE)]
    return parts[0] if len(parts) == 1 else jnp.concatenate(parts, axis=1)


def _ffn_body(x_ref, gain_ref, wg_ref, wu_ref, wd_ref, o_ref):
    x = x_ref[...]
    h = _rms_norm(x, gain_ref[...]).astype(BF16)
    acc = jnp.zeros(x.shape, F32)
    for c in range(0, D_FF, FFN_F_CHUNK):
        gate = _dot(h, wg_ref[:, c:c + FFN_F_CHUNK])
        up = _dot(h, wu_ref[:, c:c + FFN_F_CHUNK])
        act = (jax.nn.silu(gate) * up).astype(BF16)
        acc = acc + _dot(act, wd_ref[c:c + FFN_F_CHUNK, :])
    o_ref[...] = x + 0.5 * acc


def _ffn_call(x2, gain, wg, wu, wd, layer, half):
    n = x2.shape[0]
    tile = min(FFN_TILE, n)
    w_idx = lambda i: (layer, half, 0, 0)
    return pl.pallas_call(
        _ffn_body,
        grid=(n // tile,),
        in_specs=[
            pl.BlockSpec((tile, D_MODEL), lambda i: (i, 0)),
            pl.BlockSpec((None, None, 1, D_MODEL), w_idx),
            pl.BlockSpec((None, None, D_MODEL, D_FF), w_idx),
            pl.BlockSpec((None, None, D_MODEL, D_FF), w_idx),
            pl.BlockSpec((None, None, D_FF, D_MODEL), w_idx),
        ],
        out_specs=pl.BlockSpec((tile, D_MODEL), lambda i: (i, 0)),
        out_shape=jax.ShapeDtypeStruct(x2.shape, F32),
        compiler_params=pltpu.CompilerParams(
            dimension_semantics=("parallel",), vmem_limit_bytes=VMEM_LIMIT_BYTES),
        name=f"ffn_l{layer}_h{half}",
    )(x2, gain, wg, wu, wd)


def _alibi_bias(g):
    shape = (2 * 2 * WINDOW, 2 * WINDOW)
    row = lax.broadcasted_iota(jnp.int32, shape, 0)
    lane = lax.broadcasted_iota(jnp.int32, shape, 1)
    dist = (lane & (WINDOW - 1)) - (row & (2 * WINDOW - 1)) + WINDOW
    slope = (2.0 ** -(GQA_GROUPS * g + 1)) * jnp.where(lane >= WINDOW, 0.25, 1.0) * jnp.where(
        row >= 2 * WINDOW, 0.5, 1.0)
    return jnp.where((dist >= 0) & (dist < WINDOW), -slope * dist.astype(F32), -jnp.inf)


def _attention(q, kz, vt_ext, bias_ref, sinks_ref, first_key):
    n_blocks = q.shape[0] // WINDOW
    probs = [(j, g) for j in range(n_blocks) for g in range(N_KV_HEADS)]
    halves = [slice(0, 2 * WINDOW), slice(2 * WINDOW, 4 * WINDOW)]
    lane = lax.broadcasted_iota(jnp.int32, (1, 2 * WINDOW), 1)
    sinks = {(g, par): jnp.where(lane < WINDOW, sinks_ref[GQA_GROUPS * g + par], sinks_ref[GQA_GROUPS * g + 2 + par])
             for g in range(N_KV_HEADS) for par in range(2)}
    key_row = lax.broadcasted_iota(jnp.int32, (2 * 2 * WINDOW, 2 * WINDOW), 0) & (2 * WINDOW - 1)
    before_start = jnp.where(key_row >= first_key, 0.0, -jnp.inf)

    def key_window(ref, j):
        return ref[j * WINDOW:(j + 2) * WINDOW, :]

    keys = [jnp.concatenate([key_window(kz[2 * g], j), key_window(kz[2 * g + 1], j)], axis=0) for j, g in probs]
    queries = [jnp.concatenate([q[j * WINDOW:(j + 1) * WINDOW, (2 * g + pair) * WINDOW:(2 * g + pair + 1) * WINDOW]
                                for pair in range(2)], axis=0).astype(BF16) for j, g in probs]
    s = [_dot_nt(k, x) * (HEAD_DIM ** -0.5) + bias_ref[g] for k, x, (_, g) in zip(keys, queries, probs)]
    s = [x + before_start if j == 0 else x for x, (j, _) in zip(s, probs)]
    outs = {}
    sub = [(i, par) for i in range(len(probs)) for par in range(2)]
    s = [s[i][halves[par]] for i, par in sub]
    m = [jnp.maximum(jnp.max(x, axis=0, keepdims=True), sinks[probs[i][1], par]) for x, (i, par) in zip(s, sub)]
    p = [jnp.exp(x - mx) for x, mx in zip(s, m)]
    denom = [jnp.sum(x, axis=0, keepdims=True) + jnp.exp(sinks[probs[i][1], par] - mx)
             for x, mx, (i, par) in zip(p, m, sub)]
    for x, d, (i, par) in zip(p, denom, sub):
        j, g = probs[i]
        v_t = vt_ext[g * HEAD_DIM:(g + 1) * HEAD_DIM, j * WINDOW:(j + 2) * WINDOW]
        outs[j, g, par] = _dot(v_t, x.astype(BF16)) / d
    blocks = [jnp.concatenate([outs[j, g, par][:, pair * WINDOW:(pair + 1) * WINDOW]
                               for g in range(N_KV_HEADS) for pair in range(2) for par in range(2)], axis=0)
              for j in range(n_blocks)]
    return jnp.concatenate(blocks, axis=1)


def _pool_block(p_ext, pos0, pool_w_ref, pool_scale):
    tile = p_ext.shape[0] - POOL_HALO
    count = (lax.broadcasted_iota(jnp.int32, (tile, 1), 0) + pos0 + 1).astype(F32)
    outs = []
    for gi, w in enumerate(POOL_WINDOWS):
        pe = p_ext[:, gi * POOL_GROUP:(gi + 1) * POOL_GROUP]
        acc, span = pe, 1
        while span < w:
            acc = acc[span:] + acc[:-span]
            span *= 2
        win = acc[acc.shape[0] - tile:]
        pg = pe[POOL_HALO:]
        mean = win / jnp.minimum(count, float(w))
        outs.append(_dot((mean - pg).astype(BF16), pool_w_ref[gi]))
    return jnp.concatenate(outs, axis=1) * pool_scale


def _mixer_ab_body(x_ref, gain_ref, w_in_ref, qn_ref, kn_ref, sinks_ref, pool_w_ref, pool_scale_ref, w_out_ref,
                   o_ref, kz_ext, vt_ext, p_ext, bias_s):
    t = pl.program_id(1)
    tile = x_ref.shape[0]

    @pl.when((pl.program_id(0) == 0) & (t == 0))
    def _():
        for g in range(N_KV_HEADS):
            bias_s[g] = _alibi_bias(g)

    @pl.when(t == 0)
    def _():
        kz_ext[:, 0:WINDOW, :] = jnp.zeros((2 * N_KV_HEADS, WINDOW, KV_WIDTH), BF16)
        vt_ext[:, 0:WINDOW] = jnp.zeros((KV_WIDTH, WINDOW), BF16)
        p_ext[0:POOL_HALO, :] = jnp.zeros((POOL_HALO, POOL_WIDTH), F32)

    x = x_ref[...]
    h = _rms_norm(x, gain_ref[...]).astype(BF16)
    z = _dot(h, w_in_ref[...])
    q = z[:, :ATTN_WIDTH]
    k = z[:, ATTN_WIDTH:ATTN_WIDTH + KV_WIDTH]
    v = z[:, ATTN_WIDTH + KV_WIDTH:ATTN_WIDTH + 2 * KV_WIDTH]
    p = z[:, ATTN_WIDTH + 2 * KV_WIDTH:]

    hs = _head_sum_matrix(MXU_TILE, 1.0 / HEAD_DIM)
    q = q * lax.rsqrt(_head_sums(q * q, hs) + RMS_EPS) * qn_ref[...]
    k = k * lax.rsqrt(_dot((k * k).astype(BF16), hs[:KV_WIDTH, :KV_WIDTH]) + RMS_EPS) * kn_ref[...]

    k_swapped = pltpu.roll(k, HEAD_DIM, axis=1)
    low_half = lax.broadcasted_iota(jnp.int32, k.shape, 1) < HEAD_DIM
    for i, (src, keep_low) in enumerate([(k, True), (k_swapped, False), (k_swapped, True), (k, False)]):
        kz_ext[i, WINDOW:, :] = jnp.where(low_half == keep_low, src, 0.0).astype(BF16)
    vt_ext[:, WINDOW:] = v.T.astype(BF16)
    p_ext[POOL_HALO:, :] = p

    o_attn_t = _attention(q, [kz_ext.at[i] for i in range(2 * N_KV_HEADS)], vt_ext, bias_s, sinks_ref,
                          jnp.where(t == 0, WINDOW, 0))
    o_pool = _pool_block(p_ext[...], t * tile, pool_w_ref, pool_scale_ref[...])

    mixed = jnp.concatenate([o_attn_t.T, o_pool], axis=1).astype(BF16)
    o_ref[...] = x + _dot(mixed, w_out_ref[...])

    kz_ext[:, 0:WINDOW, :] = kz_ext[:, tile:tile + WINDOW, :]
    vt_ext[:, 0:WINDOW] = vt_ext[:, tile:tile + WINDOW]
    p_ext[0:POOL_HALO, :] = p_ext[tile:tile + POOL_HALO, :]


def _mixer_ab_call(x, gain, w_in, q_gain, k_gain, sinks, pool_w, pool_scale, w_out):
    b, t, _ = x.shape
    tile = min(AB_TILE, t)
    full = lambda *shape: pl.BlockSpec(shape, lambda i, j: (0,) * len(shape))
    return pl.pallas_call(
        _mixer_ab_body,
        grid=(b, t // tile),
        in_specs=[
            pl.BlockSpec((None, tile, D_MODEL), lambda i, j: (i, j, 0)),
            full(1, D_MODEL),
            full(D_MODEL, AB_IN_WIDTH),
            full(1, ATTN_WIDTH),
            full(1, KV_WIDTH),
            pl.BlockSpec(memory_space=pltpu.SMEM),
            full(len(POOL_WINDOWS), POOL_GROUP, POOL_GROUP),
            full(1, POOL_WIDTH),
            full(ATTN_WIDTH + POOL_WIDTH, D_MODEL),
        ],
        out_specs=pl.BlockSpec((None, tile, D_MODEL), lambda i, j: (i, j, 0)),
        out_shape=jax.ShapeDtypeStruct(x.shape, F32),
        scratch_shapes=[
            pltpu.VMEM((2 * N_KV_HEADS, WINDOW + tile, KV_WIDTH), BF16),
            pltpu.VMEM((KV_WIDTH, WINDOW + tile), BF16),
            pltpu.VMEM((POOL_HALO + tile, POOL_WIDTH), F32),
            pltpu.VMEM((N_KV_HEADS, 2 * 2 * WINDOW, 2 * WINDOW), F32),
        ],
        compiler_params=pltpu.CompilerParams(
            dimension_semantics=("arbitrary", "arbitrary"), vmem_limit_bytes=VMEM_LIMIT_BYTES),
        name="mixer_ab",
    )(x, gain, w_in, q_gain, k_gain, sinks, pool_w, pool_scale, w_out)


def _stack_heads(x):
    xb = x.astype(BF16)
    lane = lax.broadcasted_iota(jnp.int32, (WKV_CHUNK, 2 * RWKV_HEAD), 1)
    zeros = jnp.zeros((WKV_CHUNK, 2 * RWKV_HEAD), BF16)
    blocks = []
    for h in range(HEADS_PER_QUAD):
        pair = xb[:, (h // 2) * 2 * RWKV_HEAD:(h // 2 + 1) * 2 * RWKV_HEAD]
        keep = jnp.where((lane >= RWKV_HEAD) == (h % 2 == 1), pair, jnp.zeros((), BF16))
        blocks.append(jnp.concatenate([keep, zeros] if h < 2 else [zeros, keep], axis=1))
    return jnp.concatenate(blocks, axis=0)


def _stack_heads_transposed(x):
    pair_width = 2 * RWKV_HEAD
    row = lax.broadcasted_iota(jnp.int32, (pair_width, pair_width), 0)
    lane = lax.broadcasted_iota(jnp.int32, (pair_width, pair_width), 1)
    same_head = (row >= RWKV_HEAD) == (lane >= RWKV_HEAD)
    zeros = jnp.zeros((pair_width, pair_width), BF16)
    blocks = []
    for pair in range(HEADS_PER_QUAD // 2):
        slab = x[:, pair * pair_width:(pair + 1) * pair_width]
        slab_t = jnp.concatenate([slab, slab], axis=0).T.astype(BF16)
        keep = jnp.where(same_head, slab_t, jnp.zeros((), BF16))
        blocks.append(jnp.concatenate([keep, zeros] if pair == 0 else [zeros, keep], axis=1))
    return jnp.concatenate(blocks, axis=0)


INV_BASE = 8
WKV_STAGGER = 10


def _inverse_masks():
    row = lax.broadcasted_iota(jnp.int32, (WKV_CHUNK, MXU_TILE), 0)
    col = lax.broadcasted_iota(jnp.int32, (WKV_CHUNK, MXU_TILE), 1) & (WKV_CHUNK - 1)
    same_block = lambda size: (row // size) == (col // size)
    levels, size = [], INV_BASE
    while size < WKV_CHUNK:
        levels.append(same_block(2 * size) & jnp.logical_not(same_block(size)))
        size *= 2
    return row == col, same_block(INV_BASE), tuple(levels)


def _unit_lower_inverse(ns, stack, inv_masks):
    eye, base_blocks, levels = inv_masks
    zero = jnp.zeros((), BF16)
    mm = lambda lhs, rhs: [_dot(a, stack(b)).astype(BF16) for a, b in zip(lhs, rhs)]
    add = lambda xs, ys: [x + y for x, y in zip(xs, ys)]
    nd = [jnp.where(base_blocks, n, zero) for n in ns]
    inv = [jnp.where(eye, jnp.ones((), BF16), x) for x in nd]
    power, span = nd, 1
    while 2 * span < INV_BASE:
        power = mm(power, power)
        yield
        inv = add(inv, mm(inv, power))
        yield
        span *= 2
    for off_blocks in levels:
        n_off = [jnp.where(off_blocks, n, zero) for n in ns]
        right = mm(n_off, inv)
        yield
        inv = add(inv, mm(inv, right))
        yield
    return inv


def _wkv_operands(lw, r, k, v, kk, a, prefix):
    lw_hi = lw.astype(BF16)
    lw_lo = (lw - lw_hi.astype(F32)).astype(BF16)
    cum = _dot(prefix, lw_hi) + _dot(prefix, lw_lo)
    decay_end = jnp.exp(cum[WKV_CHUNK - 1:WKV_CHUNK, :])
    d_inv = jnp.exp(-cum)
    k_t = k * d_inv
    b_t = kk * a * d_inv
    ops = dict(
        r_t=r * jnp.exp(cum), a_t=-kk * jnp.exp(cum - lw), k_t=k_t, b_t=b_t,
        k_end=k_t * decay_end, b_end=b_t * decay_end, v=v)
    return {name: x if name in ("k_t", "b_t") else x.astype(BF16) for name, x in ops.items()}, decay_end


_DONE = object()


def _interleave(*gens):
    live = list(gens)
    while live:
        for g in list(live):
            if next(g, _DONE) is _DONE:
                live.remove(g)


def _chain(*gens):
    for g in gens:
        yield from g


def _wkv_local(chunk_inputs, masks, prefix, out):
    _, strict, incl, inv_masks = masks
    stack = _stack_heads
    quads = [slice(q * MXU_TILE, (q + 1) * MXU_TILE) for q in range(N_QUADS)]
    operands = [_wkv_operands(*inputs, prefix) for inputs in chunk_inputs]
    probs = [dict({name: x[:, lanes] for name, x in ops.items()}, decay_end=decay[:, lanes])
             for ops, decay in operands for lanes in quads]
    yield
    lhs = [jnp.concatenate([p["a_t"], p["r_t"]], axis=0) for p in probs]
    with_k = [_dot(l, _stack_heads_transposed(p["k_t"])) for l, p in zip(lhs, probs)]
    yield
    with_b = [_dot(l, _stack_heads_transposed(p["b_t"])) for l, p in zip(lhs, probs)]
    a_kv = [jnp.concatenate([jnp.where(strict, x[:WKV_CHUNK], 0.0), jnp.where(incl, x[WKV_CHUNK:], 0.0)],
                            axis=0).astype(BF16) for x in with_k]
    a_ab = [jnp.where(strict, x[:WKV_CHUNK], 0.0).astype(BF16) for x in with_b]
    a_rb = [jnp.where(incl, x[WKV_CHUNK:], 0.0).astype(BF16) for x in with_b]
    yield
    times_v = [_dot(a, stack(p["v"])) for a, p in zip(a_kv, probs)]
    yield
    inv = yield from _unit_lower_inverse(a_ab, stack, inv_masks)
    u_local = [_dot(i, stack(tv[:WKV_CHUNK])) for i, tv in zip(inv, times_v)]
    yield
    w = [_dot(i, stack(p["a_t"])) for i, p in zip(inv, probs)]
    out.extend(dict(p, wr=jnp.concatenate([wi.astype(BF16), p["r_t"]], axis=0), u=ul, y=tv[WKV_CHUNK:], a_rb=ar)
               for wi, p, ul, tv, ar in zip(w, probs, u_local, times_v, a_rb))
    yield


def _wkv_carry(chunk_rows, local, states, y_ref, masks):
    block_diag = masks[0]
    stack = _stack_heads
    quads = [slice(q * MXU_TILE, (q + 1) * MXU_TILE) for q in range(N_QUADS)]
    for i, rows in enumerate(chunk_rows):
        group = local[i * N_QUADS:(i + 1) * N_QUADS]
        from_state = [_dot_nt(p["wr"], st.astype(BF16)) for p, st in zip(group, states)]
        yield
        u = [f[:WKV_CHUNK] + p["u"] for f, p in zip(from_state, group)]
        via_u = [_dot(p["a_rb"], stack(x)) for p, x in zip(group, u)]
        yield
        outer = [_dot_tn(jnp.concatenate([p["v"], x.astype(BF16)], axis=0),
                         jnp.concatenate([p["k_end"], p["b_end"]], axis=0)) for p, x in zip(group, u)]
        for lanes, f, p, z in zip(quads, from_state, group, via_u):
            y_ref[rows, lanes] = f[WKV_CHUNK:] + p["y"] + z
        states[:] = [st * p["decay_end"] + jnp.where(block_diag, o, 0.0) for st, p, o in zip(states, group, outer)]
        yield


def _rwkv_body(x_ref, gain_ref, mu_ref, wr_ref, wk_ref, wv_ref, w0_ref, w1_ref, w2_ref, a0_ref, a1_ref, a2_ref,
               g1_ref, g2_ref, kk_ref, ka_ref, rk_ref, lnw_ref, lnb_ref, wo_ref, o_ref,
               hprev, state, h_s, xx_s, r_s, k_s, v_s, lw_s, kk_s, a_s, y_s):
    t = pl.program_id(1)
    tile = x_ref.shape[0]

    @pl.when(t == 0)
    def _():
        hprev[...] = jnp.zeros(hprev.shape, F32)
        state[...] = jnp.zeros(state.shape, F32)

    h = _rms_norm(x_ref[...], gain_ref[...])
    row = lax.broadcasted_iota(jnp.int32, (tile, 1), 0)
    shifted = jnp.where(row == 0, hprev[0:1, :], pltpu.roll(h, 1, axis=0))
    hprev[0:1, :] = h[tile - 1:tile, :]
    h_s[...] = h
    xx_s[...] = shifted - h
    mix = lambda i: (h_s[...] + xx_s[...] * mu_ref[i:i + 1, :]).astype(BF16)
    r = _dot(mix(0), wr_ref[...])
    k = _dot(mix(2), wk_ref[...])
    v = _dot(mix(3), wv_ref[...])
    w_log = -jax.nn.softplus(-(w0_ref[...] + _dot(jnp.tanh(_dot(mix(1), w1_ref[...])).astype(BF16), w2_ref[...]))) - 0.5
    a = jax.nn.sigmoid(a0_ref[...] + _dot(_dot(mix(4), a1_ref[...]).astype(BF16), a2_ref[...]))
    g = _dot(jax.nn.sigmoid(_dot(mix(5), g1_ref[...])).astype(BF16), g2_ref[...])

    hs = _head_sum_matrix(MXU_TILE, 1.0)
    kk = k * kk_ref[...]
    kk = kk * lax.rsqrt(jnp.maximum(_head_sums(kk * kk, hs), 1e-24))
    k = k * (1.0 + (a - 1.0) * ka_ref[...])
    r_s[...] = r
    k_s[...] = k
    v_s[...] = v
    lw_s[...] = -jnp.exp(w_log)
    kk_s[...] = kk
    a_s[...] = a

    rr = lax.broadcasted_iota(jnp.int32, (MXU_TILE, MXU_TILE), 0) >> 6
    cc = lax.broadcasted_iota(jnp.int32, (MXU_TILE, MXU_TILE), 1) >> 6
    block_diag = rr == cc
    ti = lax.broadcasted_iota(jnp.int32, (WKV_CHUNK, MXU_TILE), 0)
    tj = lax.broadcasted_iota(jnp.int32, (WKV_CHUNK, MXU_TILE), 1) & (WKV_CHUNK - 1)
    masks = (block_diag, tj < ti, tj <= ti, _inverse_masks())
    ci = lax.broadcasted_iota(jnp.int32, (WKV_CHUNK, WKV_CHUNK), 0)
    cj = lax.broadcasted_iota(jnp.int32, (WKV_CHUNK, WKV_CHUNK), 1)
    prefix = jnp.where(cj <= ci, 1.0, 0.0).astype(BF16)

    n_chunks = tile // WKV_CHUNK
    chunk_rows = [slice(c * WKV_CHUNK, (c + 1) * WKV_CHUNK) for c in range(n_chunks)]
    groups = [chunk_rows[:n_chunks // 2], chunk_rows[n_chunks // 2:]]
    inputs = lambda rows: (lw_s[rows, :], r_s[rows, :], k_s[rows, :], v_s[rows, :], kk_s[rows, :], a_s[rows, :])
    local = [[], []]
    first, second = (_wkv_local([inputs(rows) for rows in group], masks, prefix, out)
                     for group, out in zip(groups, local))
    states = [state[q] for q in range(N_QUADS)]
    for _ in range(WKV_STAGGER):
        next(first)
    _interleave(_chain(first, _wkv_carry(groups[0], local[0], states, y_s, masks)), second)
    _interleave(_wkv_carry(groups[1], local[1], states, y_s, masks))
    for q in range(N_QUADS):
        state[q] = states[q]

    y = y_s[...]
    r = r_s[...]
    k = k_s[...]
    v = v_s[...]
    mean = _head_sums(y, hs) * (1.0 / RWKV_HEAD)
    cen = y - mean
    var = _head_sums(cen * cen, hs) * (1.0 / RWKV_HEAD)
    yn = cen * lax.rsqrt(var + GN_EPS) * lnw_ref[...] + lnb_ref[...]
    bonus = _head_sums(r * k * rk_ref[...], hs) * v
    out = ((yn + bonus) * g).astype(BF16)
    o_ref[...] = x_ref[...] + _dot(out, wo_ref[...])


def _rwkv_call(x, gain, mu, wr, wk, wv, w0, w1, w2, a0, a1, a2, g1, g2, k_k, k_a, r_k, lnw, lnb, wo):
    b, t, _ = x.shape
    tile = min(RWKV_TILE, t)
    full = lambda *shape: pl.BlockSpec(shape, lambda i, j: (0,) * len(shape))
    vec = full(1, D_MODEL)
    sq = full(D_MODEL, D_MODEL)
    act = lambda: pltpu.VMEM((tile, D_MODEL), F32)
    return pl.pallas_call(
        _rwkv_body,
        grid=(b, t // tile),
        in_specs=[
            pl.BlockSpec((None, tile, D_MODEL), lambda i, j: (i, j, 0)),
            vec, full(mu.shape[0], D_MODEL), sq, sq, sq,
            vec, full(*w1.shape), full(*w2.shape),
            vec, full(*a1.shape), full(*a2.shape),
            full(*g1.shape), full(*g2.shape),
            vec, vec, vec, vec, vec, sq,
        ],
        out_specs=pl.BlockSpec((None, tile, D_MODEL), lambda i, j: (i, j, 0)),
        out_shape=jax.ShapeDtypeStruct(x.shape, F32),
        scratch_shapes=[
            pltpu.VMEM((8, D_MODEL), F32),
            pltpu.VMEM((N_QUADS, MXU_TILE, MXU_TILE), F32),
            act(), act(), act(), act(), act(), act(), act(), act(), act(),
        ],
        compiler_params=pltpu.CompilerParams(
            dimension_semantics=("arbitrary", "arbitrary"), vmem_limit_bytes=VMEM_LIMIT_BYTES),
        name="rwkv7",
    )(x, gain, mu, wr, wk, wv, w0, w1, w2, a0, a1, a2, g1, g2, k_k, k_a, r_k, lnw, lnb, wo)


def kernel(x, ffn_norm, ffn_w_gate, ffn_w_up, ffn_w_down, ab_norm, ab_w_in, q_norm, k_norm, attn_sinks, pool_w, pool_scale, ab_w_out, c_norm, c_mu, c_w_r, c_w_k, c_w_v, c_w0, c_w1, c_w2, c_a0, c_a1, c_a2, c_g1, c_g2, c_k_k, c_k_a, c_r_k, c_lnx_w, c_lnx_b, c_w_o):
    b, t, d = x.shape
    depth = ffn_norm.shape[0]
    bf = lambda w: w.astype(BF16)
    row = lambda p: p.reshape(1, -1).astype(F32)
    ffn_gain = ffn_norm.reshape(depth, 2, 1, d)
    wg, wu, wd = bf(ffn_w_gate), bf(ffn_w_up), bf(ffn_w_down)

    def ffn(x, layer, half):
        return _ffn_call(x.reshape(b * t, d), ffn_gain, wg, wu, wd, layer, half).reshape(b, t, d)

    for layer in range(depth):
        x = ffn(x, layer, 0)
        j = layer // 2
        if layer % 2 == 0:
            x = _mixer_ab_call(
                x, row(ab_norm[j]), bf(ab_w_in[j]), row(jnp.tile(q_norm[j], N_Q_HEADS)),
                row(jnp.tile(k_norm[j], N_KV_HEADS)), attn_sinks[j].astype(F32), bf(pool_w[j]),
                row(pool_scale[j]), bf(ab_w_out[j]))
        else:
            x = _rwkv_call(
                x, row(c_norm[j]), c_mu[j].astype(F32), bf(c_w_r[j]), bf(c_w_k[j]), bf(c_w_v[j]),
                row(c_w0[j]), bf(c_w1[j]), bf(c_w2[j]), row(c_a0[j]), bf(c_a1[j]), bf(c_a2[j]),
                bf(c_g1[j]), bf(c_g2[j]), row(c_k_k[j]), row(c_k_a[j]), row(c_r_k[j]),
                row(c_lnx_w[j]), row(c_lnx_b[j]), bf(c_w_o[j]))
        x = ffn(x, layer, 1)
    return x
```

```python
import jax
import jax.numpy as jnp
from jax import lax
from jax.experimental import pallas as pl
from jax.experimental.pallas import tpu as pltpu

F32 = jnp.float32
BF16 = jnp.bfloat16

D_MODEL = 1024
D_FF = 2816
HEAD_DIM = 64
N_Q_HEADS = 8
N_KV_HEADS = 2
GQA_GROUPS = N_Q_HEADS // N_KV_HEADS
WINDOW = 128
ATTN_WIDTH = N_Q_HEADS * HEAD_DIM
KV_WIDTH = N_KV_HEADS * HEAD_DIM
POOL_WINDOWS = (2, 4, 8, 16)
POOL_GROUP = 128
POOL_WIDTH = len(POOL_WINDOWS) * POOL_GROUP
AB_IN_WIDTH = ATTN_WIDTH + 2 * KV_WIDTH + POOL_WIDTH
POOL_HALO = 16
RWKV_HEAD = 64
assert HEAD_DIM == RWKV_HEAD
HEAD_SHIFT = HEAD_DIM.bit_length() - 1
RMS_EPS = 1e-6
GN_EPS = 64e-5

MXU_TILE = 256
HEADS_PER_QUAD = MXU_TILE // RWKV_HEAD
N_QUADS = D_MODEL // MXU_TILE
WKV_CHUNK = 64
VMEM_LIMIT_BYTES = 56 * 1024 * 1024

FFN_TILE = 512
FFN_F_CHUNK = 256
AB_TILE = 1024
RWKV_TILE = 512


def _dot(a, b):
    return jnp.dot(a, b, preferred_element_type=F32)


def _dot_nt(a, b):
    return lax.dot_general(a, b, (((1,), (1,)), ((), ())), preferred_element_type=F32)


def _dot_tn(a, b):
    return lax.dot_general(a, b, (((0,), (0,)), ((), ())), preferred_element_type=F32)


def _rms_norm(x, gain):
    return x * lax.rsqrt(jnp.mean(x * x, axis=-1, keepdims=True) + RMS_EPS) * gain


def _head_sum_matrix(width, scale):
    r = lax.broadcasted_iota(jnp.int32, (width, width), 0) >> HEAD_SHIFT
    c = lax.broadcasted_iota(jnp.int32, (width, width), 1) >> HEAD_SHIFT
    return jnp.where(r == c, scale, 0.0).astype(BF16)


def _head_sums(z, hs):
    parts = [_dot(z[:, i:i + MXU_TILE].astype(BF16), hs) for i in range(0, z.shape[1], MXU_TILE)]
    return parts[0] if len(parts) == 1 else jnp.concatenate(parts, axis=1)


def _ffn_body(x_ref, gain_ref, wg_ref, wu_ref, wd_ref, o_ref):
    x = x_ref[...]
    h = _rms_norm(x, gain_ref[...]).astype(BF16)
    acc = jnp.zeros(x.shape, F32)
    for c in range(0, D_FF, FFN_F_CHUNK):
        gate = _dot(h, wg_ref[:, c:c + FFN_F_CHUNK])
        up = _dot(h, wu_ref[:, c:c + FFN_F_CHUNK])
        act = (jax.nn.silu(gate) * up).astype(BF16)
        acc = acc + _dot(act, wd_ref[c:c + FFN_F_CHUNK, :])
    o_ref[...] = x + 0.5 * acc


def _ffn_call(x2, gain, wg, wu, wd, layer, half):
    n = x2.shape[0]
    tile = min(FFN_TILE, n)
    w_idx = lambda i: (layer, half, 0, 0)
    return pl.pallas_call(
        _ffn_body,
        grid=(n // tile,),
        in_specs=[
            pl.BlockSpec((tile, D_MODEL), lambda i: (i, 0)),
            pl.BlockSpec((None, None, 1, D_MODEL), w_idx),
            pl.BlockSpec((None, None, D_MODEL, D_FF), w_idx),
            pl.BlockSpec((None, None, D_MODEL, D_FF), w_idx),
            pl.BlockSpec((None, None, D_FF, D_MODEL), w_idx),
        ],
        out_specs=pl.BlockSpec((tile, D_MODEL), lambda i: (i, 0)),
        out_shape=jax.ShapeDtypeStruct(x2.shape, F32),
        compiler_params=pltpu.CompilerParams(
            dimension_semantics=("parallel",), vmem_limit_bytes=VMEM_LIMIT_BYTES),
        name=f"ffn_l{layer}_h{half}",
    )(x2, gain, wg, wu, wd)


def _alibi_bias(g):
    shape = (2 * 2 * WINDOW, 2 * WINDOW)
    row = lax.broadcasted_iota(jnp.int32, shape, 0)
    lane = lax.broadcasted_iota(jnp.int32, shape, 1)
    dist = (lane & (WINDOW - 1)) - (row & (2 * WINDOW - 1)) + WINDOW
    slope = (2.0 ** -(GQA_GROUPS * g + 1)) * jnp.where(lane >= WINDOW, 0.25, 1.0) * jnp.where(
        row >= 2 * WINDOW, 0.5, 1.0)
    return jnp.where((dist >= 0) & (dist < WINDOW), -slope * dist.astype(F32), -jnp.inf)


def _attention(q, kz, vt_ext, bias_ref, sinks_ref, first_key):
    n_blocks = q.shape[0] // WINDOW
    probs = [(j, g) for j in range(n_blocks) for g in range(N_KV_HEADS)]
    halves = [slice(0, 2 * WINDOW), slice(2 * WINDOW, 4 * WINDOW)]
    lane = lax.broadcasted_iota(jnp.int32, (1, 2 * WINDOW), 1)
    sinks = {(g, par): jnp.where(lane < WINDOW, sinks_ref[GQA_GROUPS * g + par], sinks_ref[GQA_GROUPS * g + 2 + par])
             for g in range(N_KV_HEADS) for par in range(2)}
    key_row = lax.broadcasted_iota(jnp.int32, (2 * 2 * WINDOW, 2 * WINDOW), 0) & (2 * WINDOW - 1)
    before_start = jnp.where(key_row >= first_key, 0.0, -jnp.inf)

    def key_window(ref, j):
        return ref[j * WINDOW:(j + 2) * WINDOW, :]

    keys = [jnp.concatenate([key_window(kz[2 * g], j), key_window(kz[2 * g + 1], j)], axis=0) for j, g in probs]
    queries = [jnp.concatenate([q[j * WINDOW:(j + 1) * WINDOW, (2 * g + pair) * WINDOW:(2 * g + pair + 1) * WINDOW]
                                for pair in range(2)], axis=0).astype(BF16) for j, g in probs]
    s = [_dot_nt(k, x) * (HEAD_DIM ** -0.5) + bias_ref[g] for k, x, (_, g) in zip(keys, queries, probs)]
    s = [x + before_start if j == 0 else x for x, (j, _) in zip(s, probs)]
    outs = {}
    sub = [(i, par) for i in range(len(probs)) for par in range(2)]
    s = [s[i][halves[par]] for i, par in sub]
    m = [jnp.maximum(jnp.max(x, axis=0, keepdims=True), sinks[probs[i][1], par]) for x, (i, par) in zip(s, sub)]
    p = [jnp.exp(x - mx) for x, mx in zip(s, m)]
    denom = [jnp.sum(x, axis=0, keepdims=True) + jnp.exp(sinks[probs[i][1], par] - mx)
             for x, mx, (i, par) in zip(p, m, sub)]
    for x, d, (i, par) in zip(p, denom, sub):
        j, g = probs[i]
        v_t = vt_ext[g * HEAD_DIM:(g + 1) * HEAD_DIM, j * WINDOW:(j + 2) * WINDOW]
        outs[j, g, par] = _dot(v_t, x.astype(BF16)) / d
    blocks = [jnp.concatenate([outs[j, g, par][:, pair * WINDOW:(pair + 1) * WINDOW]
                               for g in range(N_KV_HEADS) for pair in range(2) for par in range(2)], axis=0)
              for j in range(n_blocks)]
    return jnp.concatenate(blocks, axis=1)


def _pool_block(p_ext, pos0, pool_w_ref, pool_scale):
    tile = p_ext.shape[0] - POOL_HALO
    count = (lax.broadcasted_iota(jnp.int32, (tile, 1), 0) + pos0 + 1).astype(F32)
    outs = []
    for gi, w in enumerate(POOL_WINDOWS):
        pe = p_ext[:, gi * POOL_GROUP:(gi + 1) * POOL_GROUP]
        acc, span = pe, 1
        while span < w:
            acc = acc[span:] + acc[:-span]
            span *= 2
        win = acc[acc.shape[0] - tile:]
        pg = pe[POOL_HALO:]
        mean = win / jnp.minimum(count, float(w))
        outs.append(_dot((mean - pg).astype(BF16), pool_w_ref[gi]))
    return jnp.concatenate(outs, axis=1) * pool_scale


def _mixer_ab_body(x_ref, gain_ref, w_in_ref, qn_ref, kn_ref, sinks_ref, pool_w_ref, pool_scale_ref, w_out_ref,
                   o_ref, kz_ext, vt_ext, p_ext, bias_s):
    t = pl.program_id(1)
    tile = x_ref.shape[0]

    @pl.when((pl.program_id(0) == 0) & (t == 0))
    def _():
        for g in range(N_KV_HEADS):
            bias_s[g] = _alibi_bias(g)

    @pl.when(t == 0)
    def _():
        kz_ext[:, 0:WINDOW, :] = jnp.zeros((2 * N_KV_HEADS, WINDOW, KV_WIDTH), BF16)
        vt_ext[:, 0:WINDOW] = jnp.zeros((KV_WIDTH, WINDOW), BF16)
        p_ext[0:POOL_HALO, :] = jnp.zeros((POOL_HALO, POOL_WIDTH), F32)

    x = x_ref[...]
    h = _rms_norm(x, gain_ref[...]).astype(BF16)
    z = _dot(h, w_in_ref[...])
    q = z[:, :ATTN_WIDTH]
    k = z[:, ATTN_WIDTH:ATTN_WIDTH + KV_WIDTH]
    v = z[:, ATTN_WIDTH + KV_WIDTH:ATTN_WIDTH + 2 * KV_WIDTH]
    p = z[:, ATTN_WIDTH + 2 * KV_WIDTH:]

    hs = _head_sum_matrix(MXU_TILE, 1.0 / HEAD_DIM)
    q = q * lax.rsqrt(_head_sums(q * q, hs) + RMS_EPS) * qn_ref[...]
    k = k * lax.rsqrt(_dot((k * k).astype(BF16), hs[:KV_WIDTH, :KV_WIDTH]) + RMS_EPS) * kn_ref[...]

    k_swapped = pltpu.roll(k, HEAD_DIM, axis=1)
    low_half = lax.broadcasted_iota(jnp.int32, k.shape, 1) < HEAD_DIM
    for i, (src, keep_low) in enumerate([(k, True), (k_swapped, False), (k_swapped, True), (k, False)]):
        kz_ext[i, WINDOW:, :] = jnp.where(low_half == keep_low, src, 0.0).astype(BF16)
    vt_ext[:, WINDOW:] = v.T.astype(BF16)
    p_ext[POOL_HALO:, :] = p

    o_attn_t = _attention(q, [kz_ext.at[i] for i in range(2 * N_KV_HEADS)], vt_ext, bias_s, sinks_ref,
                          jnp.where(t == 0, WINDOW, 0))
    o_pool = _pool_block(p_ext[...], t * tile, pool_w_ref, pool_scale_ref[...])

    mixed = jnp.concatenate([o_attn_t.T, o_pool], axis=1).astype(BF16)
    o_ref[...] = x + _dot(mixed, w_out_ref[...])

    kz_ext[:, 0:WINDOW, :] = kz_ext[:, tile:tile + WINDOW, :]
    vt_ext[:, 0:WINDOW] = vt_ext[:, tile:tile + WINDOW]
    p_ext[0:POOL_HALO, :] = p_ext[tile:tile + POOL_HALO, :]


def _mixer_ab_call(x, gain, w_in, q_gain, k_gain, sinks, pool_w, pool_scale, w_out):
    b, t, _ = x.shape
    tile = min(AB_TILE, t)
    full = lambda *shape: pl.BlockSpec(shape, lambda i, j: (0,) * len(shape))
    return pl.pallas_call(
        _mixer_ab_body,
        grid=(b, t // tile),
        in_specs=[
            pl.BlockSpec((None, tile, D_MODEL), lambda i, j: (i, j, 0)),
            full(1, D_MODEL),
            full(D_MODEL, AB_IN_WIDTH),
            full(1, ATTN_WIDTH),
            full(1, KV_WIDTH),
            pl.BlockSpec(memory_space=pltpu.SMEM),
            full(len(POOL_WINDOWS), POOL_GROUP, POOL_GROUP),
            full(1, POOL_WIDTH),
            full(ATTN_WIDTH + POOL_WIDTH, D_MODEL),
        ],
        out_specs=pl.BlockSpec((None, tile, D_MODEL), lambda i, j: (i, j, 0)),
        out_shape=jax.ShapeDtypeStruct(x.shape, F32),
        scratch_shapes=[
            pltpu.VMEM((2 * N_KV_HEADS, WINDOW + tile, KV_WIDTH), BF16),
            pltpu.VMEM((KV_WIDTH, WINDOW + tile), BF16),
            pltpu.VMEM((POOL_HALO + tile, POOL_WIDTH), F32),
            pltpu.VMEM((N_KV_HEADS, 2 * 2 * WINDOW, 2 * WINDOW), F32),
        ],
        compiler_params=pltpu.CompilerParams(
            dimension_semantics=("arbitrary", "arbitrary"), vmem_limit_bytes=VMEM_LIMIT_BYTES),
        name="mixer_ab",
    )(x, gain, w_in, q_gain, k_gain, sinks, pool_w, pool_scale, w_out)


def _stack_heads(x):
    xb = x.astype(BF16)
    lane = lax.broadcasted_iota(jnp.int32, (WKV_CHUNK, 2 * RWKV_HEAD), 1)
    zeros = jnp.zeros((WKV_CHUNK, 2 * RWKV_HEAD), BF16)
    blocks = []
    for h in range(HEADS_PER_QUAD):
        pair = xb[:, (h // 2) * 2 * RWKV_HEAD:(h // 2 + 1) * 2 * RWKV_HEAD]
        keep = jnp.where((lane >= RWKV_HEAD) == (h % 2 == 1), pair, jnp.zeros((), BF16))
        blocks.append(jnp.concatenate([keep, zeros] if h < 2 else [zeros, keep], axis=1))
    return jnp.concatenate(blocks, axis=0)


def _stack_heads_transposed(x):
    pair_width = 2 * RWKV_HEAD
    row = lax.broadcasted_iota(jnp.int32, (pair_width, pair_width), 0)
    lane = lax.broadcasted_iota(jnp.int32, (pair_width, pair_width), 1)
    same_head = (row >= RWKV_HEAD) == (lane >= RWKV_HEAD)
    zeros = jnp.zeros((pair_width, pair_width), BF16)
    blocks = []
    for pair in range(HEADS_PER_QUAD // 2):
        slab = x[:, pair * pair_width:(pair + 1) * pair_width]
        slab_t = jnp.concatenate([slab, slab], axis=0).T.astype(BF16)
        keep = jnp.where(same_head, slab_t, jnp.zeros((), BF16))
        blocks.append(jnp.concatenate([keep, zeros] if pair == 0 else [zeros, keep], axis=1))
    return jnp.concatenate(blocks, axis=0)


INV_BASE = 8
WKV_STAGGER = 10


def _inverse_masks():
    row = lax.broadcasted_iota(jnp.int32, (WKV_CHUNK, MXU_TILE), 0)
    col = lax.broadcasted_iota(jnp.int32, (WKV_CHUNK, MXU_TILE), 1) & (WKV_CHUNK - 1)
    same_block = lambda size: (row // size) == (col // size)
    levels, size = [], INV_BASE
    while size < WKV_CHUNK:
        levels.append(same_block(2 * size) & jnp.logical_not(same_block(size)))
        size *= 2
    return row == col, same_block(INV_BASE), tuple(levels)


def _unit_lower_inverse(ns, stack, inv_masks):
    eye, base_blocks, levels = inv_masks
    zero = jnp.zeros((), BF16)
    mm = lambda lhs, rhs: [_dot(a, stack(b)).astype(BF16) for a, b in zip(lhs, rhs)]
    add = lambda xs, ys: [x + y for x, y in zip(xs, ys)]
    nd = [jnp.where(base_blocks, n, zero) for n in ns]
    inv = [jnp.where(eye, jnp.ones((), BF16), x) for x in nd]
    power, span = nd, 1
    while 2 * span < INV_BASE:
        power = mm(power, power)
        yield
        inv = add(inv, mm(inv, power))
        yield
        span *= 2
    for off_blocks in levels:
        n_off = [jnp.where(off_blocks, n, zero) for n in ns]
        right = mm(n_off, inv)
        yield
        inv = add(inv, mm(inv, right))
        yield
    return inv


def _wkv_operands(lw, r, k, v, kk, a, prefix):
    lw_hi = lw.astype(BF16)
    lw_lo = (lw - lw_hi.astype(F32)).astype(BF16)
    cum = _dot(prefix, lw_hi) + _dot(prefix, lw_lo)
    decay_end = jnp.exp(cum[WKV_CHUNK - 1:WKV_CHUNK, :])
    d_inv = jnp.exp(-cum)
    k_t = k * d_inv
    b_t = kk * a * d_inv
    ops = dict(
        r_t=r * jnp.exp(cum), a_t=-kk * jnp.exp(cum - lw), k_t=k_t, b_t=b_t,
        k_end=k_t * decay_end, b_end=b_t * decay_end, v=v)
    return {name: x if name in ("k_t", "b_t") else x.astype(BF16) for name, x in ops.items()}, decay_end


_DONE = object()


def _interleave(*gens):
    live = list(gens)
    while live:
        for g in list(live):
            if next(g, _DONE) is _DONE:
                live.remove(g)


def _chain(*gens):
    for g in gens:
        yield from g


def _wkv_local(chunk_inputs, masks, prefix, out):
    _, strict, incl, inv_masks = masks
    stack = _stack_heads
    quads = [slice(q * MXU_TILE, (q + 1) * MXU_TILE) for q in range(N_QUADS)]
    operands = [_wkv_operands(*inputs, prefix) for inputs in chunk_inputs]
    probs = [dict({name: x[:, lanes] for name, x in ops.items()}, decay_end=decay[:, lanes])
             for ops, decay in operands for lanes in quads]
    yield
    lhs = [jnp.concatenate([p["a_t"], p["r_t"]], axis=0) for p in probs]
    with_k = [_dot(l, _stack_heads_transposed(p["k_t"])) for l, p in zip(lhs, probs)]
    yield
    with_b = [_dot(l, _stack_heads_transposed(p["b_t"])) for l, p in zip(lhs, probs)]
    a_kv = [jnp.concatenate([jnp.where(strict, x[:WKV_CHUNK], 0.0), jnp.where(incl, x[WKV_CHUNK:], 0.0)],
                            axis=0).astype(BF16) for x in with_k]
    a_ab = [jnp.where(strict, x[:WKV_CHUNK], 0.0).astype(BF16) for x in with_b]
    a_rb = [jnp.where(incl, x[WKV_CHUNK:], 0.0).astype(BF16) for x in with_b]
    yield
    times_v = [_dot(a, stack(p["v"])) for a, p in zip(a_kv, probs)]
    yield
    inv = yield from _unit_lower_inverse(a_ab, stack, inv_masks)
    u_local = [_dot(i, stack(tv[:WKV_CHUNK])) for i, tv in zip(inv, times_v)]
    yield
    w = [_dot(i, stack(p["a_t"])) for i, p in zip(inv, probs)]
    out.extend(dict(p, wr=jnp.concatenate([wi.astype(BF16), p["r_t"]], axis=0), u=ul, y=tv[WKV_CHUNK:], a_rb=ar)
               for wi, p, ul, tv, ar in zip(w, probs, u_local, times_v, a_rb))
    yield


def _wkv_carry(chunk_rows, local, states, y_ref, masks):
    block_diag = masks[0]
    stack = _stack_heads
    quads = [slice(q * MXU_TILE, (q + 1) * MXU_TILE) for q in range(N_QUADS)]
    for i, rows in enumerate(chunk_rows):
        group = local[i * N_QUADS:(i + 1) * N_QUADS]
        from_state = [_dot_nt(p["wr"], st.astype(BF16)) for p, st in zip(group, states)]
        yield
        u = [f[:WKV_CHUNK] + p["u"] for f, p in zip(from_state, group)]
        via_u = [_dot(p["a_rb"], stack(x)) for p, x in zip(group, u)]
        yield
        outer = [_dot_tn(jnp.concatenate([p["v"], x.astype(BF16)], axis=0),
                         jnp.concatenate([p["k_end"], p["b_end"]], axis=0)) for p, x in zip(group, u)]
        for lanes, f, p, z in zip(quads, from_state, group, via_u):
            y_ref[rows, lanes] = f[WKV_CHUNK:] + p["y"] + z
        states[:] = [st * p["decay_end"] + jnp.where(block_diag, o, 0.0) for st, p, o in zip(states, group, outer)]
        yield


def _rwkv_body(x_ref, gain_ref, mu_ref, wr_ref, wk_ref, wv_ref, w0_ref, w1_ref, w2_ref, a0_ref, a1_ref, a2_ref,
               g1_ref, g2_ref, kk_ref, ka_ref, rk_ref, lnw_ref, lnb_ref, wo_ref, o_ref,
               hprev, state, h_s, xx_s, r_s, k_s, v_s, lw_s, kk_s, a_s, y_s):
    t = pl.program_id(1)
    tile = x_ref.shape[0]

    @pl.when(t == 0)
    def _():
        hprev[...] = jnp.zeros(hprev.shape, F32)
        state[...] = jnp.zeros(state.shape, F32)

    h = _rms_norm(x_ref[...], gain_ref[...])
    row = lax.broadcasted_iota(jnp.int32, (tile, 1), 0)
    shifted = jnp.where(row == 0, hprev[0:1, :], pltpu.roll(h, 1, axis=0))
    hprev[0:1, :] = h[tile - 1:tile, :]
    h_s[...] = h
    xx_s[...] = shifted - h
    mix = lambda i: (h_s[...] + xx_s[...] * mu_ref[i:i + 1, :]).astype(BF16)
    r = _dot(mix(0), wr_ref[...])
    k = _dot(mix(2), wk_ref[...])
    v = _dot(mix(3), wv_ref[...])
    w_log = -jax.nn.softplus(-(w0_ref[...] + _dot(jnp.tanh(_dot(mix(1), w1_ref[...])).astype(BF16), w2_ref[...]))) - 0.5
    a = jax.nn.sigmoid(a0_ref[...] + _dot(_dot(mix(4), a1_ref[...]).astype(BF16), a2_ref[...]))
    g = _dot(jax.nn.sigmoid(_dot(mix(5), g1_ref[...])).astype(BF16), g2_ref[...])

    hs = _head_sum_matrix(MXU_TILE, 1.0)
    kk = k * kk_ref[...]
    kk = kk * lax.rsqrt(jnp.maximum(_head_sums(kk * kk, hs), 1e-24))
    k = k * (1.0 + (a - 1.0) * ka_ref[...])
    r_s[...] = r
    k_s[...] = k
    v_s[...] = v
    lw_s[...] = -jnp.exp(w_log)
    kk_s[...] = kk
    a_s[...] = a

    rr = lax.broadcasted_iota(jnp.int32, (MXU_TILE, MXU_TILE), 0) >> HEAD_SHIFT
    cc = lax.broadcasted_iota(jnp.int32, (MXU_TILE, MXU_TILE), 1) >> HEAD_SHIFT
    block_diag = rr == cc
    ti = lax.broadcasted_iota(jnp.int32, (WKV_CHUNK, MXU_TILE), 0)
    tj = lax.broadcasted_iota(jnp.int32, (WKV_CHUNK, MXU_TILE), 1) & (WKV_CHUNK - 1)
    masks = (block_diag, tj < ti, tj <= ti, _inverse_masks())
    ci = lax.broadcasted_iota(jnp.int32, (WKV_CHUNK, WKV_CHUNK), 0)
    cj = lax.broadcasted_iota(jnp.int32, (WKV_CHUNK, WKV_CHUNK), 1)
    prefix = jnp.where(cj <= ci, 1.0, 0.0).astype(BF16)

    n_chunks = tile // WKV_CHUNK
    chunk_rows = [slice(c * WKV_CHUNK, (c + 1) * WKV_CHUNK) for c in range(n_chunks)]
    groups = [chunk_rows[:n_chunks // 2], chunk_rows[n_chunks // 2:]]
    inputs = lambda rows: (lw_s[rows, :], r_s[rows, :], k_s[rows, :], v_s[rows, :], kk_s[rows, :], a_s[rows, :])
    local = [[], []]
    first, second = (_wkv_local([inputs(rows) for rows in group], masks, prefix, out)
                     for group, out in zip(groups, local))
    states = [state[q] for q in range(N_QUADS)]
    for _ in range(WKV_STAGGER):
        next(first)
    _interleave(_chain(first, _wkv_carry(groups[0], local[0], states, y_s, masks)), second)
    _interleave(_wkv_carry(groups[1], local[1], states, y_s, masks))
    for q in range(N_QUADS):
        state[q] = states[q]

    y = y_s[...]
    r = r_s[...]
    k = k_s[...]
    v = v_s[...]
    mean = _head_sums(y, hs) * (1.0 / RWKV_HEAD)
    cen = y - mean
    var = _head_sums(cen * cen, hs) * (1.0 / RWKV_HEAD)
    yn = cen * lax.rsqrt(var + GN_EPS) * lnw_ref[...] + lnb_ref[...]
    bonus = _head_sums(r * k * rk_ref[...], hs) * v
    out = ((yn + bonus) * g).astype(BF16)
    o_ref[...] = x_ref[...] + _dot(out, wo_ref[...])


def _rwkv_call(x, gain, mu, wr, wk, wv, w0, w1, w2, a0, a1, a2, g1, g2, k_k, k_a, r_k, lnw, lnb, wo):
    b, t, _ = x.shape
    tile = min(RWKV_TILE, t)
    full = lambda *shape: pl.BlockSpec(shape, lambda i, j: (0,) * len(shape))
    vec = full(1, D_MODEL)
    sq = full(D_MODEL, D_MODEL)
    act = lambda: pltpu.VMEM((tile, D_MODEL), F32)
    return pl.pallas_call(
        _rwkv_body,
        grid=(b, t // tile),
        in_specs=[
            pl.BlockSpec((None, tile, D_MODEL), lambda i, j: (i, j, 0)),
            vec, full(mu.shape[0], D_MODEL), sq, sq, sq,
            vec, full(*w1.shape), full(*w2.shape),
            vec, full(*a1.shape), full(*a2.shape),
            full(*g1.shape), full(*g2.shape),
            vec, vec, vec, vec, vec, sq,
        ],
        out_specs=pl.BlockSpec((None, tile, D_MODEL), lambda i, j: (i, j, 0)),
        out_shape=jax.ShapeDtypeStruct(x.shape, F32),
        scratch_shapes=[
            pltpu.VMEM((8, D_MODEL), F32),
            pltpu.VMEM((N_QUADS, MXU_TILE, MXU_TILE), F32),
            act(), act(), act(), act(), act(), act(), act(), act(), act(),
        ],
        compiler_params=pltpu.CompilerParams(
            dimension_semantics=("arbitrary", "arbitrary"), vmem_limit_bytes=VMEM_LIMIT_BYTES),
        name="rwkv7",
    )(x, gain, mu, wr, wk, wv, w0, w1, w2, a0, a1, a2, g1, g2, k_k, k_a, r_k, lnw, lnb, wo)


def kernel(x, ffn_norm, ffn_w_gate, ffn_w_up, ffn_w_down, ab_norm, ab_w_in, q_norm, k_norm, attn_sinks, pool_w, pool_scale, ab_w_out, c_norm, c_mu, c_w_r, c_w_k, c_w_v, c_w0, c_w1, c_w2, c_a0, c_a1, c_a2, c_g1, c_g2, c_k_k, c_k_a, c_r_k, c_lnx_w, c_lnx_b, c_w_o):
    b, t, d = x.shape
    depth = ffn_norm.shape[0]
    bf = lambda w: w.astype(BF16)
    row = lambda p: p.reshape(1, -1).astype(F32)
    ffn_gain = ffn_norm.reshape(depth, 2, 1, d)
    wg, wu, wd = bf(ffn_w_gate), bf(ffn_w_up), bf(ffn_w_down)

    def ffn(x, layer, half):
        return _ffn_call(x.reshape(b * t, d), ffn_gain, wg, wu, wd, layer, half).reshape(b, t, d)

    for layer in range(depth):
        x = ffn(x, layer, 0)
        j = layer // 2
        if layer % 2 == 0:
            x = _mixer_ab_call(
                x, row(ab_norm[j]), bf(ab_w_in[j]), row(jnp.tile(q_norm[j], N_Q_HEADS)),
                row(jnp.tile(k_norm[j], N_KV_HEADS)), attn_sinks[j].astype(F32), bf(pool_w[j]),
                row(pool_scale[j]), bf(ab_w_out[j]))
        else:
            x = _rwkv_call(
                x, row(c_norm[j]), c_mu[j].astype(F32), bf(c_w_r[j]), bf(c_w_k[j]), bf(c_w_v[j]),
                row(c_w0[j]), bf(c_w1[j]), bf(c_w2[j]), row(c_a0[j]), bf(c_a1[j]), bf(c_a2[j]),
                bf(c_g1[j]), bf(c_g2[j]), row(c_k_k[j]), row(c_k_a[j]), row(c_r_k[j]),
                row(c_lnx_w[j]), row(c_lnx_b[j]), bf(c_w_o[j]))
        x = ffn(x, layer, 1)
    return x
```

```python
import jax
import jax.numpy as jnp
from jax import lax
from jax.experimental import pallas as pl
from jax.experimental.pallas import tpu as pltpu

F32 = jnp.float32
BF16 = jnp.bfloat16

D_MODEL = 1024
D_FF = 2816
HEAD_DIM = 64
N_Q_HEADS = 8
N_KV_HEADS = 2
GQA_GROUPS = N_Q_HEADS // N_KV_HEADS
WINDOW = 128
ATTN_WIDTH = N_Q_HEADS * HEAD_DIM
KV_WIDTH = N_KV_HEADS * HEAD_DIM
POOL_WINDOWS = (2, 4, 8, 16)
POOL_GROUP = 128
POOL_WIDTH = len(POOL_WINDOWS) * POOL_GROUP
AB_IN_WIDTH = ATTN_WIDTH + 2 * KV_WIDTH + POOL_WIDTH
POOL_HALO = 16
RWKV_HEAD = 64
assert HEAD_DIM == RWKV_HEAD
HEAD_SHIFT = HEAD_DIM.bit_length() - 1
RMS_EPS = 1e-6
GN_EPS = 64e-5

MXU_TILE = 256
HEADS_PER_QUAD = MXU_TILE // RWKV_HEAD
N_QUADS = D_MODEL // MXU_TILE
WKV_CHUNK = 64
VMEM_LIMIT_BYTES = 56 * 1024 * 1024

FFN_TILE = 512
FFN_F_CHUNK = 256
AB_TILE = 1024
RWKV_TILE = 512


def _dot(a, b):
    return jnp.dot(a, b, preferred_element_type=F32)


def _dot_nt(a, b):
    return lax.dot_general(a, b, (((1,), (1,)), ((), ())), preferred_element_type=F32)


def _dot_tn(a, b):
    return lax.dot_general(a, b, (((0,), (0,)), ((), ())), preferred_element_type=F32)


def _rms_norm(x, gain):
    return x * lax.rsqrt(jnp.mean(x * x, axis=-1, keepdims=True) + RMS_EPS) * gain


def _head_sum_matrix(width, scale):
    r = lax.broadcasted_iota(jnp.int32, (width, width), 0) >> HEAD_SHIFT
    c = lax.broadcasted_iota(jnp.int32, (width, width), 1) >> HEAD_SHIFT
    return jnp.where(r == c, scale, 0.0).astype(BF16)


def _head_sums(z, hs):
    parts = [_dot(z[:, i:i + MXU_TILE].astype(BF16), hs) for i in range(0, z.shape[1], MXU_TILE)]
    return parts[0] if len(parts) == 1 else jnp.concatenate(parts, axis=1)


def _ffn_body(x_ref, gain_ref, wg_ref, wu_ref, wd_ref, o_ref):
    x = x_ref[...]
    h = _rms_norm(x, gain_ref[...]).astype(BF16)
    acc = jnp.zeros(x.shape, F32)
    for c in range(0, D_FF, FFN_F_CHUNK):
        gate = _dot(h, wg_ref[:, c:c + FFN_F_CHUNK])
        up = _dot(h, wu_ref[:, c:c + FFN_F_CHUNK])
        act = (jax.nn.silu(gate) * up).astype(BF16)
        acc = acc + _dot(act, wd_ref[c:c + FFN_F_CHUNK, :])
    o_ref[...] = x + 0.5 * acc


def _ffn_call(x2, gain, wg, wu, wd, layer, half):
    n = x2.shape[0]
    tile = min(FFN_TILE, n)
    w_idx = lambda i: (layer, half, 0, 0)
    return pl.pallas_call(
        _ffn_body,
        grid=(n // tile,),
        in_specs=[
            pl.BlockSpec((tile, D_MODEL), lambda i: (i, 0)),
            pl.BlockSpec((None, None, 1, D_MODEL), w_idx),
            pl.BlockSpec((None, None, D_MODEL, D_FF), w_idx),
            pl.BlockSpec((None, None, D_MODEL, D_FF), w_idx),
            pl.BlockSpec((None, None, D_FF, D_MODEL), w_idx),
        ],
        out_specs=pl.BlockSpec((tile, D_MODEL), lambda i: (i, 0)),
        out_shape=jax.ShapeDtypeStruct(x2.shape, F32),
        compiler_params=pltpu.CompilerParams(
            dimension_semantics=("parallel",), vmem_limit_bytes=VMEM_LIMIT_BYTES),
        name=f"ffn_l{layer}_h{half}",
    )(x2, gain, wg, wu, wd)


def _alibi_bias(g):
    shape = (2 * 2 * WINDOW, 2 * WINDOW)
    row = lax.broadcasted_iota(jnp.int32, shape, 0)
    lane = lax.broadcasted_iota(jnp.int32, shape, 1)
    dist = (lane & (WINDOW - 1)) - (row & (2 * WINDOW - 1)) + WINDOW
    slope = (2.0 ** -(GQA_GROUPS * g + 1)) * jnp.where(lane >= WINDOW, 0.25, 1.0) * jnp.where(
        row >= 2 * WINDOW, 0.5, 1.0)
    return jnp.where((dist >= 0) & (dist < WINDOW), -slope * dist.astype(F32), -jnp.inf)


def _attention(q, kz, vt_ext, bias_ref, sinks_ref, first_key):
    n_blocks = q.shape[0] // WINDOW
    probs = [(j, g) for j in range(n_blocks) for g in range(N_KV_HEADS)]
    halves = [slice(0, 2 * WINDOW), slice(2 * WINDOW, 4 * WINDOW)]
    lane = lax.broadcasted_iota(jnp.int32, (1, 2 * WINDOW), 1)
    sinks = {(g, par): jnp.where(lane < WINDOW, sinks_ref[GQA_GROUPS * g + par], sinks_ref[GQA_GROUPS * g + 2 + par])
             for g in range(N_KV_HEADS) for par in range(2)}
    key_row = lax.broadcasted_iota(jnp.int32, (2 * 2 * WINDOW, 2 * WINDOW), 0) & (2 * WINDOW - 1)
    before_start = jnp.where(key_row >= first_key, 0.0, -jnp.inf)

    def key_window(ref, j):
        return ref[j * WINDOW:(j + 2) * WINDOW, :]

    keys = [jnp.concatenate([key_window(kz[2 * g], j), key_window(kz[2 * g + 1], j)], axis=0) for j, g in probs]
    queries = [jnp.concatenate([q[j * WINDOW:(j + 1) * WINDOW, (2 * g + pair) * WINDOW:(2 * g + pair + 1) * WINDOW]
                                for pair in range(2)], axis=0).astype(BF16) for j, g in probs]
    s = [_dot_nt(k, x) * (HEAD_DIM ** -0.5) + bias_ref[g] for k, x, (_, g) in zip(keys, queries, probs)]
    s = [x + before_start if j == 0 else x for x, (j, _) in zip(s, probs)]
    outs = {}
    sub = [(i, par) for i in range(len(probs)) for par in range(2)]
    s = [s[i][halves[par]] for i, par in sub]
    m = [jnp.maximum(jnp.max(x, axis=0, keepdims=True), sinks[probs[i][1], par]) for x, (i, par) in zip(s, sub)]
    p = [jnp.exp(x - mx) for x, mx in zip(s, m)]
    denom = [jnp.sum(x, axis=0, keepdims=True) + jnp.exp(sinks[probs[i][1], par] - mx)
             for x, mx, (i, par) in zip(p, m, sub)]
    for x, d, (i, par) in zip(p, denom, sub):
        j, g = probs[i]
        v_t = vt_ext[g * HEAD_DIM:(g + 1) * HEAD_DIM, j * WINDOW:(j + 2) * WINDOW]
        outs[j, g, par] = _dot(v_t, x.astype(BF16)) / d
    blocks = [jnp.concatenate([outs[j, g, par][:, pair * WINDOW:(pair + 1) * WINDOW]
                               for g in range(N_KV_HEADS) for pair in range(2) for par in range(2)], axis=0)
              for j in range(n_blocks)]
    return jnp.concatenate(blocks, axis=1)


def _pool_block(p_ext, pos0, pool_w_ref, pool_scale):
    tile = p_ext.shape[0] - POOL_HALO
    count = (lax.broadcasted_iota(jnp.int32, (tile, 1), 0) + pos0 + 1).astype(F32)
    outs = []
    for gi, w in enumerate(POOL_WINDOWS):
        pe = p_ext[:, gi * POOL_GROUP:(gi + 1) * POOL_GROUP]
        acc, span = pe, 1
        while span < w:
            acc = acc[span:] + acc[:-span]
            span *= 2
        win = acc[acc.shape[0] - tile:]
        pg = pe[POOL_HALO:]
        mean = win / jnp.minimum(count, float(w))
        outs.append(_dot((mean - pg).astype(BF16), pool_w_ref[gi]))
    return jnp.concatenate(outs, axis=1) * pool_scale


def _mixer_ab_body(x_ref, gain_ref, w_in_ref, qn_ref, kn_ref, sinks_ref, pool_w_ref, pool_scale_ref, w_out_ref,
                   o_ref, kz_ext, vt_ext, p_ext, bias_s):
    t = pl.program_id(1)
    tile = x_ref.shape[0]

    @pl.when((pl.program_id(0) == 0) & (t == 0))
    def _():
        for g in range(N_KV_HEADS):
            bias_s[g] = _alibi_bias(g)

    @pl.when(t == 0)
    def _():
        kz_ext[:, 0:WINDOW, :] = jnp.zeros((2 * N_KV_HEADS, WINDOW, KV_WIDTH), BF16)
        vt_ext[:, 0:WINDOW] = jnp.zeros((KV_WIDTH, WINDOW), BF16)
        p_ext[0:POOL_HALO, :] = jnp.zeros((POOL_HALO, POOL_WIDTH), F32)

    x = x_ref[...]
    h = _rms_norm(x, gain_ref[...]).astype(BF16)
    z = _dot(h, w_in_ref[...])
    q = z[:, :ATTN_WIDTH]
    k = z[:, ATTN_WIDTH:ATTN_WIDTH + KV_WIDTH]
    v = z[:, ATTN_WIDTH + KV_WIDTH:ATTN_WIDTH + 2 * KV_WIDTH]
    p = z[:, ATTN_WIDTH + 2 * KV_WIDTH:]

    hs = _head_sum_matrix(MXU_TILE, 1.0 / HEAD_DIM)
    q = q * lax.rsqrt(_head_sums(q * q, hs) + RMS_EPS) * qn_ref[...]
    k = k * lax.rsqrt(_dot((k * k).astype(BF16), hs[:KV_WIDTH, :KV_WIDTH]) + RMS_EPS) * kn_ref[...]

    k_swapped = pltpu.roll(k, HEAD_DIM, axis=1)
    low_half = lax.broadcasted_iota(jnp.int32, k.shape, 1) < HEAD_DIM
    for i, (src, keep_low) in enumerate([(k, True), (k_swapped, False), (k_swapped, True), (k, False)]):
        kz_ext[i, WINDOW:, :] = jnp.where(low_half == keep_low, src, 0.0).astype(BF16)
    vt_ext[:, WINDOW:] = v.T.astype(BF16)
    p_ext[POOL_HALO:, :] = p

    o_attn_t = _attention(q, [kz_ext.at[i] for i in range(2 * N_KV_HEADS)], vt_ext, bias_s, sinks_ref,
                          jnp.where(t == 0, WINDOW, 0))
    o_pool = _pool_block(p_ext[...], t * tile, pool_w_ref, pool_scale_ref[...])

    mixed = jnp.concatenate([o_attn_t.T, o_pool], axis=1).astype(BF16)
    o_ref[...] = x + _dot(mixed, w_out_ref[...])

    kz_ext[:, 0:WINDOW, :] = kz_ext[:, tile:tile + WINDOW, :]
    vt_ext[:, 0:WINDOW] = vt_ext[:, tile:tile + WINDOW]
    p_ext[0:POOL_HALO, :] = p_ext[tile:tile + POOL_HALO, :]


def _mixer_ab_call(x, gain, w_in, q_gain, k_gain, sinks, pool_w, pool_scale, w_out):
    b, t, _ = x.shape
    tile = min(AB_TILE, t)
    full = lambda *shape: pl.BlockSpec(shape, lambda i, j: (0,) * len(shape))
    return pl.pallas_call(
        _mixer_ab_body,
        grid=(b, t // tile),
        in_specs=[
            pl.BlockSpec((None, tile, D_MODEL), lambda i, j: (i, j, 0)),
            full(1, D_MODEL),
            full(D_MODEL, AB_IN_WIDTH),
            full(1, ATTN_WIDTH),
            full(1, KV_WIDTH),
            pl.BlockSpec(memory_space=pltpu.SMEM),
            full(len(POOL_WINDOWS), POOL_GROUP, POOL_GROUP),
            full(1, POOL_WIDTH),
            full(ATTN_WIDTH + POOL_WIDTH, D_MODEL),
        ],
        out_specs=pl.BlockSpec((None, tile, D_MODEL), lambda i, j: (i, j, 0)),
        out_shape=jax.ShapeDtypeStruct(x.shape, F32),
        scratch_shapes=[
            pltpu.VMEM((2 * N_KV_HEADS, WINDOW + tile, KV_WIDTH), BF16),
            pltpu.VMEM((KV_WIDTH, WINDOW + tile), BF16),
            pltpu.VMEM((POOL_HALO + tile, POOL_WIDTH), F32),
            pltpu.VMEM((N_KV_HEADS, 2 * 2 * WINDOW, 2 * WINDOW), F32),
        ],
        compiler_params=pltpu.CompilerParams(
            dimension_semantics=("arbitrary", "arbitrary"), vmem_limit_bytes=VMEM_LIMIT_BYTES),
        name="mixer_ab",
    )(x, gain, w_in, q_gain, k_gain, sinks, pool_w, pool_scale, w_out)


def _stack_heads(x):
    xb = x.astype(BF16)
    lane = lax.broadcasted_iota(jnp.int32, (WKV_CHUNK, 2 * RWKV_HEAD), 1)
    zeros = jnp.zeros((WKV_CHUNK, 2 * RWKV_HEAD), BF16)
    blocks = []
    for h in range(HEADS_PER_QUAD):
        pair = xb[:, (h // 2) * 2 * RWKV_HEAD:(h // 2 + 1) * 2 * RWKV_HEAD]
        keep = jnp.where((lane >= RWKV_HEAD) == (h % 2 == 1), pair, jnp.zeros((), BF16))
        blocks.append(jnp.concatenate([keep, zeros] if h < 2 else [zeros, keep], axis=1))
    return jnp.concatenate(blocks, axis=0)


def _stack_heads_transposed(x):
    pair_width = 2 * RWKV_HEAD
    row = lax.broadcasted_iota(jnp.int32, (pair_width, pair_width), 0)
    lane = lax.broadcasted_iota(jnp.int32, (pair_width, pair_width), 1)
    same_head = (row >= RWKV_HEAD) == (lane >= RWKV_HEAD)
    zeros = jnp.zeros((pair_width, pair_width), BF16)
    blocks = []
    for pair in range(HEADS_PER_QUAD // 2):
        slab = x[:, pair * pair_width:(pair + 1) * pair_width]
        slab_t = jnp.concatenate([slab, slab], axis=0).T.astype(BF16)
        keep = jnp.where(same_head, slab_t, jnp.zeros((), BF16))
        blocks.append(jnp.concatenate([keep, zeros] if pair == 0 else [zeros, keep], axis=1))
    return jnp.concatenate(blocks, axis=0)


INV_BASE = 8
WKV_STAGGER = 10


def _inverse_masks():
    row = lax.broadcasted_iota(jnp.int32, (WKV_CHUNK, MXU_TILE), 0)
    col = lax.broadcasted_iota(jnp.int32, (WKV_CHUNK, MXU_TILE), 1) & (WKV_CHUNK - 1)
    same_block = lambda size: (row // size) == (col // size)
    levels, size = [], INV_BASE
    while size < WKV_CHUNK:
        levels.append(same_block(2 * size) & jnp.logical_not(same_block(size)))
        size *= 2
    return row == col, same_block(INV_BASE), tuple(levels)


def _unit_lower_inverse(ns, stack, inv_masks):
    eye, base_blocks, levels = inv_masks
    zero = jnp.zeros((), BF16)
    mm = lambda lhs, rhs: [_dot(a, stack(b)).astype(BF16) for a, b in zip(lhs, rhs)]
    add = lambda xs, ys: [x + y for x, y in zip(xs, ys)]
    nd = [jnp.where(base_blocks, n, zero) for n in ns]
    inv = [jnp.where(eye, jnp.ones((), BF16), x) for x in nd]
    power, span = nd, 1
    while 2 * span < INV_BASE:
        power = mm(power, power)
        yield
        inv = add(inv, mm(inv, power))
        yield
        span *= 2
    for off_blocks in levels:
        n_off = [jnp.where(off_blocks, n, zero) for n in ns]
        right = mm(n_off, inv)
        yield
        inv = add(inv, mm(inv, right))
        yield
    return inv


def _wkv_operands(lw, r, k, v, kk, a, prefix):
    lw_hi = lw.astype(BF16)
    lw_lo = (lw - lw_hi.astype(F32)).astype(BF16)
    cum = _dot(prefix, lw_hi) + _dot(prefix, lw_lo)
    decay_end = jnp.exp(cum[WKV_CHUNK - 1:WKV_CHUNK, :])
    d_inv = jnp.exp(-cum)
    k_t = k * d_inv
    b_t = kk * a * d_inv
    ops = dict(
        r_t=r * jnp.exp(cum), a_t=-kk * jnp.exp(cum - lw), k_t=k_t, b_t=b_t,
        k_end=k_t * decay_end, b_end=b_t * decay_end, v=v)
    return {name: x if name in ("k_t", "b_t") else x.astype(BF16) for name, x in ops.items()}, decay_end


_DONE = object()


def _interleave(*gens):
    live = list(gens)
    while live:
        for g in list(live):
            if next(g, _DONE) is _DONE:
                live.remove(g)


def _chain(*gens):
    for g in gens:
        yield from g


def _wkv_local(chunk_inputs, masks, prefix, out):
    _, strict, incl, inv_masks = masks
    stack = _stack_heads
    quads = [slice(q * MXU_TILE, (q + 1) * MXU_TILE) for q in range(N_QUADS)]
    operands = [_wkv_operands(*inputs, prefix) for inputs in chunk_inputs]
    probs = [dict({name: x[:, lanes] for name, x in ops.items()}, decay_end=decay[:, lanes])
             for ops, decay in operands for lanes in quads]
    yield
    lhs = [jnp.concatenate([p["a_t"], p["r_t"]], axis=0) for p in probs]
    with_k = [_dot(l, _stack_heads_transposed(p["k_t"])) for l, p in zip(lhs, probs)]
    yield
    with_b = [_dot(l, _stack_heads_transposed(p["b_t"])) for l, p in zip(lhs, probs)]
    a_kv = [jnp.concatenate([jnp.where(strict, x[:WKV_CHUNK], 0.0), jnp.where(incl, x[WKV_CHUNK:], 0.0)],
                            axis=0).astype(BF16) for x in with_k]
    a_ab = [jnp.where(strict, x[:WKV_CHUNK], 0.0).astype(BF16) for x in with_b]
    a_rb = [jnp.where(incl, x[WKV_CHUNK:], 0.0).astype(BF16) for x in with_b]
    yield
    times_v = [_dot(a, stack(p["v"])) for a, p in zip(a_kv, probs)]
    yield
    inv = yield from _unit_lower_inverse(a_ab, stack, inv_masks)
    u_local = [_dot(i, stack(tv[:WKV_CHUNK])) for i, tv in zip(inv, times_v)]
    yield
    w = [_dot(i, stack(p["a_t"])) for i, p in zip(inv, probs)]
    out.extend(dict(p, wr=jnp.concatenate([wi.astype(BF16), p["r_t"]], axis=0), u=ul, y=tv[WKV_CHUNK:], a_rb=ar)
               for wi, p, ul, tv, ar in zip(w, probs, u_local, times_v, a_rb))
    yield


def _wkv_carry(chunk_rows, local, states, y_ref, masks):
    block_diag = masks[0]
    stack = _stack_heads
    quads = [slice(q * MXU_TILE, (q + 1) * MXU_TILE) for q in range(N_QUADS)]
    for i, rows in enumerate(chunk_rows):
        group = local[i * N_QUADS:(i + 1) * N_QUADS]
        from_state = [_dot_nt(p["wr"], st.astype(BF16)) for p, st in zip(group, states)]
        yield
        u = [f[:WKV_CHUNK] + p["u"] for f, p in zip(from_state, group)]
        via_u = [_dot(p["a_rb"], stack(x)) for p, x in zip(group, u)]
        yield
        outer = [_dot_tn(jnp.concatenate([p["v"], x.astype(BF16)], axis=0),
                         jnp.concatenate([p["k_end"], p["b_end"]], axis=0)) for p, x in zip(group, u)]
        for lanes, f, p, z in zip(quads, from_state, group, via_u):
            y_ref[rows, lanes] = f[WKV_CHUNK:] + p["y"] + z
        states[:] = [st * p["decay_end"] + jnp.where(block_diag, o, 0.0) for st, p, o in zip(states, group, outer)]
        yield


RWKV_VECTORS = ("gain", "w0", "a0", "k_k", "k_a", "r_k", "lnw", "lnb")


def _rwkv_body(x_ref, vec_ref, wr_ref, wk_ref, wv_ref, w1_ref, w2_ref, a1_ref, a2_ref, g1_ref, g2_ref, wo_ref, o_ref,
               hprev, state, h_s, xx_s, r_s, k_s, v_s, lw_s, kk_s, a_s, y_s):
    t = pl.program_id(1)
    tile = x_ref.shape[0]
    gain, w0, a0, k_k, k_a, r_k, lnw, lnb = (vec_ref[i:i + 1, :] for i in range(len(RWKV_VECTORS)))
    mu = lambda i: vec_ref[len(RWKV_VECTORS) + i:len(RWKV_VECTORS) + i + 1, :]

    @pl.when(t == 0)
    def _():
        hprev[...] = jnp.zeros(hprev.shape, F32)
        state[...] = jnp.zeros(state.shape, F32)

    h = _rms_norm(x_ref[...], gain)
    row = lax.broadcasted_iota(jnp.int32, (tile, 1), 0)
    shifted = jnp.where(row == 0, hprev[0:1, :], pltpu.roll(h, 1, axis=0))
    hprev[0:1, :] = h[tile - 1:tile, :]
    h_s[...] = h
    xx_s[...] = shifted - h
    mix = lambda i: (h_s[...] + xx_s[...] * mu(i)).astype(BF16)
    r = _dot(mix(0), wr_ref[...])
    k = _dot(mix(2), wk_ref[...])
    v = _dot(mix(3), wv_ref[...])
    w_log = -jax.nn.softplus(-(w0 + _dot(jnp.tanh(_dot(mix(1), w1_ref[...])).astype(BF16), w2_ref[...]))) - 0.5
    a = jax.nn.sigmoid(a0 + _dot(_dot(mix(4), a1_ref[...]).astype(BF16), a2_ref[...]))
    g = _dot(jax.nn.sigmoid(_dot(mix(5), g1_ref[...])).astype(BF16), g2_ref[...])

    hs = _head_sum_matrix(MXU_TILE, 1.0)
    kk = k * k_k
    kk = kk * lax.rsqrt(jnp.maximum(_head_sums(kk * kk, hs), 1e-24))
    k = k * (1.0 + (a - 1.0) * k_a)
    r_s[...] = r
    k_s[...] = k
    v_s[...] = v
    lw_s[...] = -jnp.exp(w_log)
    kk_s[...] = kk
    a_s[...] = a

    rr = lax.broadcasted_iota(jnp.int32, (MXU_TILE, MXU_TILE), 0) >> HEAD_SHIFT
    cc = lax.broadcasted_iota(jnp.int32, (MXU_TILE, MXU_TILE), 1) >> HEAD_SHIFT
    block_diag = rr == cc
    ti = lax.broadcasted_iota(jnp.int32, (WKV_CHUNK, MXU_TILE), 0)
    tj = lax.broadcasted_iota(jnp.int32, (WKV_CHUNK, MXU_TILE), 1) & (WKV_CHUNK - 1)
    masks = (block_diag, tj < ti, tj <= ti, _inverse_masks())
    ci = lax.broadcasted_iota(jnp.int32, (WKV_CHUNK, WKV_CHUNK), 0)
    cj = lax.broadcasted_iota(jnp.int32, (WKV_CHUNK, WKV_CHUNK), 1)
    prefix = jnp.where(cj <= ci, 1.0, 0.0).astype(BF16)

    n_chunks = tile // WKV_CHUNK
    chunk_rows = [slice(c * WKV_CHUNK, (c + 1) * WKV_CHUNK) for c in range(n_chunks)]
    groups = [chunk_rows[:n_chunks // 2], chunk_rows[n_chunks // 2:]]
    inputs = lambda rows: (lw_s[rows, :], r_s[rows, :], k_s[rows, :], v_s[rows, :], kk_s[rows, :], a_s[rows, :])
    local = [[], []]
    first, second = (_wkv_local([inputs(rows) for rows in group], masks, prefix, out)
                     for group, out in zip(groups, local))
    states = [state[q] for q in range(N_QUADS)]
    for _ in range(WKV_STAGGER):
        next(first)
    _interleave(_chain(first, _wkv_carry(groups[0], local[0], states, y_s, masks)), second)
    _interleave(_wkv_carry(groups[1], local[1], states, y_s, masks))
    for q in range(N_QUADS):
        state[q] = states[q]

    y = y_s[...]
    r = r_s[...]
    k = k_s[...]
    v = v_s[...]
    mean = _head_sums(y, hs) * (1.0 / RWKV_HEAD)
    cen = y - mean
    var = _head_sums(cen * cen, hs) * (1.0 / RWKV_HEAD)
    yn = cen * lax.rsqrt(var + GN_EPS) * lnw + lnb
    bonus = _head_sums(r * k * r_k, hs) * v
    out = ((yn + bonus) * g).astype(BF16)
    o_ref[...] = x_ref[...] + _dot(out, wo_ref[...])


def _rwkv_call(x, gain, mu, wr, wk, wv, w0, w1, w2, a0, a1, a2, g1, g2, k_k, k_a, r_k, lnw, lnb, wo):
    b, t, _ = x.shape
    tile = min(RWKV_TILE, t)
    full = lambda *shape: pl.BlockSpec(shape, lambda i, j: (0,) * len(shape))
    sq = full(D_MODEL, D_MODEL)
    act = lambda: pltpu.VMEM((tile, D_MODEL), F32)
    vectors = jnp.concatenate([gain, w0, a0, k_k, k_a, r_k, lnw, lnb, mu], axis=0)
    return pl.pallas_call(
        _rwkv_body,
        grid=(b, t // tile),
        in_specs=[
            pl.BlockSpec((None, tile, D_MODEL), lambda i, j: (i, j, 0)),
            full(*vectors.shape), sq, sq, sq,
            full(*w1.shape), full(*w2.shape), full(*a1.shape), full(*a2.shape),
            full(*g1.shape), full(*g2.shape), sq,
        ],
        out_specs=pl.BlockSpec((None, tile, D_MODEL), lambda i, j: (i, j, 0)),
        out_shape=jax.ShapeDtypeStruct(x.shape, F32),
        scratch_shapes=[
            pltpu.VMEM((8, D_MODEL), F32),
            pltpu.VMEM((N_QUADS, MXU_TILE, MXU_TILE), F32),
            act(), act(), act(), act(), act(), act(), act(), act(), act(),
        ],
        compiler_params=pltpu.CompilerParams(
            dimension_semantics=("arbitrary", "arbitrary"), vmem_limit_bytes=VMEM_LIMIT_BYTES),
        name="rwkv7",
    )(x, vectors, wr, wk, wv, w1, w2, a1, a2, g1, g2, wo)


def kernel(x, ffn_norm, ffn_w_gate, ffn_w_up, ffn_w_down, ab_norm, ab_w_in, q_norm, k_norm, attn_sinks, pool_w, pool_scale, ab_w_out, c_norm, c_mu, c_w_r, c_w_k, c_w_v, c_w0, c_w1, c_w2, c_a0, c_a1, c_a2, c_g1, c_g2, c_k_k, c_k_a, c_r_k, c_lnx_w, c_lnx_b, c_w_o):
    b, t, d = x.shape
    depth = ffn_norm.shape[0]
    bf = lambda w: w.astype(BF16)
    row = lambda p: p.reshape(1, -1).astype(F32)
    ffn_gain = ffn_norm.reshape(depth, 2, 1, d)
    wg, wu, wd = bf(ffn_w_gate), bf(ffn_w_up), bf(ffn_w_down)

    def ffn(x, layer, half):
        return _ffn_call(x.reshape(b * t, d), ffn_gain, wg, wu, wd, layer, half).reshape(b, t, d)

    for layer in range(depth):
        x = ffn(x, layer, 0)
        j = layer // 2
        if layer % 2 == 0:
            x = _mixer_ab_call(
                x, row(ab_norm[j]), bf(ab_w_in[j]), row(jnp.tile(q_norm[j], N_Q_HEADS)),
                row(jnp.tile(k_norm[j], N_KV_HEADS)), attn_sinks[j].astype(F32), bf(pool_w[j]),
                row(pool_scale[j]), bf(ab_w_out[j]))
        else:
            x = _rwkv_call(
                x, row(c_norm[j]), c_mu[j].astype(F32), bf(c_w_r[j]), bf(c_w_k[j]), bf(c_w_v[j]),
                row(c_w0[j]), bf(c_w1[j]), bf(c_w2[j]), row(c_a0[j]), bf(c_a1[j]), bf(c_a2[j]),
                bf(c_g1[j]), bf(c_g2[j]), row(c_k_k[j]), row(c_k_a[j]), row(c_r_k[j]),
                row(c_lnx_w[j]), row(c_lnx_b[j]), bf(c_w_o[j]))
        x = ffn(x, layer, 1)
    return x
```
